```python
import jax, jax.numpy as jnp
from jax import lax
import numpy as np

D_MODEL = 2048
BATCH = 4
SEQ = 2048
DEPTH = 2

CHUNK = 64
N_LEFT_CHUNKS = 8
BAND = (N_LEFT_CHUNKS + 1) * CHUNK
HEAD_DIM = 64
N_HEADS = D_MODEL // HEAD_DIM
D_FF = 4 * D_MODEL
N_A_LAYERS = DEPTH // 2
N_B_LAYERS = DEPTH - N_A_LAYERS
DECAY_LORA = 96
ICLR_LORA = 96
GATE_LORA = 256
REL_CLIP = 256
RMS_EPS = 1e-6
GN_EPS = 64e-5

kernel_name = "rwkv7_yoco_chunk_band_attn_adaln"


def rms_norm(x, g):
    xf = x.astype(jnp.float32)
    y = xf * lax.rsqrt(jnp.mean(xf * xf, axis=-1, keepdims=True) + RMS_EPS)
    return (y * g.astype(jnp.float32)).astype(x.dtype)


def modulate(h, shift, scale):
    return h * (1 + scale[:, None, :]) + shift[:, None, :]


def token_shift(h):
    return jnp.pad(h, ((0, 0), (1, 0), (0, 0)))[:, :-1]


def rwkv7_time_mix(h, mu, w_r, w_k, w_v, w_o, w0, w1, w2, a0, a1, a2,
                   g1, g2, k_k, k_a, r_k, ln_w, ln_b):
    B, T, D = h.shape
    xx = token_shift(h) - h
    xr, xw, xk, xv, xa, xg = [h + xx * mu[j] for j in range(6)]
    r = xr @ w_r
    k = xk @ w_k
    v = xv @ w_v
    w_log = -jax.nn.softplus(-(w0 + jnp.tanh(xw @ w1) @ w2)) - 0.5
    a = jax.nn.sigmoid(a0 + (xa @ a1) @ a2)
    g = jax.nn.sigmoid(xg @ g1) @ g2

    def heads(t):
        return t.reshape(B, T, N_HEADS, HEAD_DIM).astype(jnp.float32)

    kk = heads(k * k_k)
    kk = kk / jnp.maximum(jnp.sqrt(jnp.sum(kk * kk, axis=-1, keepdims=True)), 1e-12)
    k = k * (1 + (a - 1) * k_a)
    r_h, k_h, v_h, a_h = heads(r), heads(k), heads(v), heads(a)
    decay = jnp.exp(-jnp.exp(heads(w_log)))

    def step(S, inp):
        r_t, w_t, k_t, v_t, kk_t, a_t = inp
        sa = jnp.einsum('bhvk,bhk->bhv', S, -kk_t)
        S = (S * w_t[:, :, None, :] + sa[..., None] * (kk_t * a_t)[:, :, None, :]
             + v_t[..., None] * k_t[:, :, None, :])
        y = jnp.einsum('bhvk,bhk->bhv', S, r_t)
        return S, y

    S0 = jnp.zeros((B, N_HEADS, HEAD_DIM, HEAD_DIM), jnp.float32)
    xs = tuple(jnp.moveaxis(t, 1, 0) for t in (r_h, decay, k_h, v_h, kk, a_h))
    _, y = lax.scan(step, S0, xs)
    y = jnp.moveaxis(y, 0, 1)
    mean = jnp.mean(y, axis=-1, keepdims=True)
    var = jnp.mean(jnp.square(y - mean), axis=-1, keepdims=True)
    y = ((y - mean) * lax.rsqrt(var + GN_EPS) * ln_w.reshape(N_HEADS, HEAD_DIM).astype(jnp.float32)
         + ln_b.reshape(N_HEADS, HEAD_DIM).astype(jnp.float32))
    y = y + jnp.sum(r_h * k_h * r_k.astype(jnp.float32), axis=-1, keepdims=True) * v_h
    y = y.reshape(B, T, D).astype(h.dtype) * g
    return y @ w_o


def chunk_band_attention(h, kp, vp, w_q, w_o, rel_bias):
    B, T, D = h.shape
    nc = T // CHUNK
    q = (h @ w_q).reshape(B, nc, CHUNK, N_HEADS, HEAD_DIM)
    q_chunks = jnp.moveaxis(q, 1, 0)
    pad = N_LEFT_CHUNKS * CHUNK
    i = jnp.arange(CHUNK)[:, None]
    j = jnp.arange(BAND)[None, :]
    rel = i + pad - j
    idx = jnp.clip(rel, -REL_CLIP, REL_CLIP) + REL_CLIP
    bias = rel_bias[:, idx].astype(jnp.float32)
    scale = HEAD_DIM ** -0.5

    def one_chunk(args):
        n, q_c = args
        start = n * CHUNK
        k_b = lax.dynamic_slice_in_dim(kp, start, BAND, axis=1)
        v_b = lax.dynamic_slice_in_dim(vp, start, BAND, axis=1)
        s = jnp.einsum('bqhd,bkhd->bhqk', q_c, k_b).astype(jnp.float32) * scale + bias
        valid = (start + jnp.arange(BAND)) >= pad
        s = jnp.where(valid[None, None, None, :], s, -1e30)
        p = jax.nn.softmax(s, axis=-1).astype(v_b.dtype)
        return jnp.einsum('bhqk,bkhd->bqhd', p, v_b)

    o = lax.map(one_chunk, (jnp.arange(nc), q_chunks))
    o = jnp.moveaxis(o, 0, 1).reshape(B, T, D)
    return o @ w_o


def sq_relu_mlp(h, w_up, w_down):
    return jnp.square(jax.nn.relu(h @ w_up)) @ w_down


def setup_inputs(seed: int = 0) -> dict:
    key = jax.random.key(seed)
    ks = iter(jax.random.split(key, 64))
    D = D_MODEL
    nrm = lambda shape, s: jax.random.normal(next(ks), shape, jnp.float32) * s
    nA, nB, L = N_A_LAYERS, N_B_LAYERS, DEPTH
    return {
        "x": nrm((BATCH, SEQ, D), 1.0),
        "c": nrm((BATCH, D), 1.0),
        "w_ada": nrm((L, D, 6 * D), 0.5 * D ** -0.5),
        "b_ada": nrm((L, 6 * D), 0.01),
        "g_mix": 1.0 + nrm((L, D), 0.02),
        "g_mlp": 1.0 + nrm((L, D), 0.02),
        "w_up": nrm((L, D, D_FF), D ** -0.5),
        "w_down": nrm((L, D_FF, D), D_FF ** -0.5),
        "rwkv_mu": jax.random.uniform(next(ks), (nA, 6, D), jnp.float32),
        "rwkv_w_r": nrm((nA, D, D), D ** -0.5),
        "rwkv_w_k": nrm((nA, D, D), D ** -0.5),
        "rwkv_w_v": nrm((nA, D, D), D ** -0.5),
        "rwkv_w_o": nrm((nA, D, D), D ** -0.5),
        "rwkv_w0": -1.0 + nrm((nA, D), 0.5),
        "rwkv_w1": nrm((nA, D, DECAY_LORA), D ** -0.5),
        "rwkv_w2": nrm((nA, DECAY_LORA, D), 0.5 * DECAY_LORA ** -0.5),
        "rwkv_a0": nrm((nA, D), 0.5),
        "rwkv_a1": nrm((nA, D, ICLR_LORA), D ** -0.5),
        "rwkv_a2": nrm((nA, ICLR_LORA, D), 0.5 * ICLR_LORA ** -0.5),
        "rwkv_g1": nrm((nA, D, GATE_LORA), D ** -0.5),
        "rwkv_g2": nrm((nA, GATE_LORA, D), GATE_LORA ** -0.5),
        "rwkv_k_k": 0.85 + nrm((nA, D), 0.05),
        "rwkv_k_a": 1.0 + nrm((nA, D), 0.05),
        "rwkv_r_k": nrm((nA, N_HEADS, HEAD_DIM), 0.1),
        "rwkv_ln_w": 1.0 + nrm((nA, D), 0.02),
        "rwkv_ln_b": nrm((nA, D), 0.01),
        "attn_w_q": nrm((nB, D, D), D ** -0.5),
        "attn_w_o": nrm((nB, D, D), D ** -0.5),
        "attn_rel_bias": nrm((nB, N_HEADS, 2 * REL_CLIP + 1), 0.5),
        "w_ada_kv": nrm((D, 2 * D), 0.5 * D ** -0.5),
        "b_ada_kv": nrm((2 * D,), 0.01),
        "g_kv": 1.0 + nrm((D,), 0.02),
        "w_k_shared": nrm((D, D), D ** -0.5),
        "w_v_shared": nrm((D, D), D ** -0.5),
        "g_final": 1.0 + nrm((D,), 0.02),
    }


def reference(x, c, w_ada, b_ada, g_mix, g_mlp, w_up, w_down,
              rwkv_mu, rwkv_w_r, rwkv_w_k, rwkv_w_v, rwkv_w_o,
              rwkv_w0, rwkv_w1, rwkv_w2, rwkv_a0, rwkv_a1, rwkv_a2,
              rwkv_g1, rwkv_g2, rwkv_k_k, rwkv_k_a, rwkv_r_k, rwkv_ln_w, rwkv_ln_b,
              attn_w_q, attn_w_o, attn_rel_bias,
              w_ada_kv, b_ada_kv, g_kv, w_k_shared, w_v_shared, g_final):
    B, T, D = x.shape
    pad = N_LEFT_CHUNKS * CHUNK
    kp = vp = None
    for layer in range(DEPTH):
        mod = c @ w_ada[layer] + b_ada[layer]
        sh1, sc1, gt1, sh2, sc2, gt2 = jnp.split(mod, 6, axis=-1)
        if layer == N_A_LAYERS:
            sh_kv, sc_kv = jnp.split(c @ w_ada_kv + b_ada_kv, 2, axis=-1)
            h_kv = modulate(rms_norm(x, g_kv), sh_kv, sc_kv)
            k_s = (h_kv @ w_k_shared).reshape(B, T, N_HEADS, HEAD_DIM)
            v_s = (h_kv @ w_v_shared).reshape(B, T, N_HEADS, HEAD_DIM)
            kp = jnp.pad(k_s, ((0, 0), (pad, 0), (0, 0), (0, 0)))
            vp = jnp.pad(v_s, ((0, 0), (pad, 0), (0, 0), (0, 0)))
        h = modulate(rms_norm(x, g_mix[layer]), sh1, sc1)
        if layer < N_A_LAYERS:
            i = layer
            y = rwkv7_time_mix(h, rwkv_mu[i], rwkv_w_r[i], rwkv_w_k[i], rwkv_w_v[i], rwkv_w_o[i],
                               rwkv_w0[i], rwkv_w1[i], rwkv_w2[i], rwkv_a0[i], rwkv_a1[i], rwkv_a2[i],
                               rwkv_g1[i], rwkv_g2[i], rwkv_k_k[i], rwkv_k_a[i], rwkv_r_k[i],
                               rwkv_ln_w[i], rwkv_ln_b[i])
        else:
            i = layer - N_A_LAYERS
            y = chunk_band_attention(h, kp, vp, attn_w_q[i], attn_w_o[i], attn_rel_bias[i])
        x = x + gt1[:, None, :] * y
        h = modulate(rms_norm(x, g_mlp[layer]), sh2, sc2)
        x = x + gt2[:, None, :] * sq_relu_mlp(h, w_up[layer], w_down[layer])
    return rms_norm(x, g_final)
```

```python
import functools

import jax
import jax.numpy as jnp
from jax import lax
from jax.experimental import pallas as pl
from jax.experimental.pallas import tpu as pltpu

F32 = jnp.float32
BF16 = jnp.bfloat16

CHUNK = 64
N_LEFT_CHUNKS = 8
HEAD_DIM = 64
REL_CLIP = 256
RMS_EPS = 1e-6
GN_EPS = 64e-5
MASK_VALUE = -1e30

LANES = 128
ATTN_Q = 2 * CHUNK
ATTN_PAD = N_LEFT_CHUNKS * CHUNK
ATTN_W = ATTN_PAD + ATTN_Q
SCAN_TB = 16
VMEM_LIMIT = 48 * 1024 * 1024


def _cparams(sem):
    return pltpu.CompilerParams(dimension_semantics=sem, vmem_limit_bytes=VMEM_LIMIT)


def _ada_kernel(c_ref, w_ref, b_ref, o_ref):
    w = w_ref[0].astype(BF16)
    o_ref[0] = jnp.dot(c_ref[...], w, preferred_element_type=F32) + b_ref[0]


def ada_mod(c_pad, w, b, tn=1024):
    L, D, N = w.shape
    R = c_pad.shape[0]
    return pl.pallas_call(
        _ada_kernel,
        out_shape=jax.ShapeDtypeStruct((L, R, N), F32),
        grid=(L, N // tn),
        in_specs=[
            pl.BlockSpec((R, D), lambda l, j: (0, 0)),
            pl.BlockSpec((1, D, tn), lambda l, j: (l, 0, j)),
            pl.BlockSpec((1, 1, tn), lambda l, j: (l, 0, j)),
        ],
        out_specs=pl.BlockSpec((1, R, tn), lambda l, j: (l, 0, j)),
        compiler_params=_cparams(("arbitrary", "arbitrary")),
        name="ada_mod",
    )(c_pad, w, b.reshape(L, 1, N))


def _rms(x, g):
    ms = jnp.mean(x * x, axis=-1, keepdims=True)
    return x * lax.rsqrt(ms + RMS_EPS) * g


def _norm_mod_kernel(x_ref, g_ref, sh_ref, sc_ref, o_ref):
    y = _rms(x_ref[...], g_ref[...])
    o_ref[...] = (y * (1 + sc_ref[0]) + sh_ref[0]).astype(o_ref.dtype)


def norm_mod(x, g, sh, sc, rows_per_batch, tm=512):
    M, D = x.shape
    per = rows_per_batch // tm
    return pl.pallas_call(
        _norm_mod_kernel,
        out_shape=jax.ShapeDtypeStruct((M, D), BF16),
        grid=(M // tm,),
        in_specs=[
            pl.BlockSpec((tm, D), lambda i: (i, 0)),
            pl.BlockSpec((1, D), lambda i: (0, 0)),
            pl.BlockSpec((1, 1, D), lambda i: (i // per, 0, 0)),
            pl.BlockSpec((1, 1, D), lambda i: (i // per, 0, 0)),
        ],
        out_specs=pl.BlockSpec((tm, D), lambda i: (i, 0)),
        compiler_params=_cparams(("arbitrary",)),
        name="norm_mod",
    )(x, g.reshape(1, D), sh, sc)


def _final_norm_kernel(x_ref, g_ref, o_ref):
    o_ref[...] = _rms(x_ref[...], g_ref[...])


def final_norm(x, g, tm=512):
    M, D = x.shape
    return pl.pallas_call(
        _final_norm_kernel,
        out_shape=jax.ShapeDtypeStruct((M, D), F32),
        grid=(M // tm,),
        in_specs=[
            pl.BlockSpec((tm, D), lambda i: (i, 0)),
            pl.BlockSpec((1, D), lambda i: (0, 0)),
        ],
        out_specs=pl.BlockSpec((tm, D), lambda i: (i, 0)),
        compiler_params=_cparams(("arbitrary",)),
        name="final_norm",
    )(x, g.reshape(1, D))


def _premix_kernel(per, x_ref, xp_ref, g_ref, sh_ref, sc_ref, mu_ref, *o_refs):
    g = g_ref[...]
    sc = 1 + sc_ref[0]
    sh = sh_ref[0]
    h = _rms(x_ref[...], g) * sc + sh
    hp = (_rms(xp_ref[...], g) * sc + sh)[7:8, :]
    first_tile = (pl.program_id(0) % per) == 0
    hp = jnp.where(first_tile, 0.0, hp)
    row = lax.broadcasted_iota(jnp.int32, h.shape, 0)
    hs = jnp.where(row == 0, hp, pltpu.roll(h, 1, axis=0))
    xx = hs - h
    for j, o_ref in enumerate(o_refs):
        o_ref[...] = (h + xx * mu_ref[j:j + 1, :]).astype(o_ref.dtype)


def premix(x, g, sh, sc, mu, rows_per_batch, tm=256):
    M, D = x.shape
    per = rows_per_batch // tm
    sub = 8
    n_mix = mu.shape[0]
    return pl.pallas_call(
        functools.partial(_premix_kernel, per),
        out_shape=[jax.ShapeDtypeStruct((M, D), BF16)] * n_mix,
        grid=(M // tm,),
        in_specs=[
            pl.BlockSpec((tm, D), lambda i: (i, 0)),
            pl.BlockSpec((sub, D), lambda i: (jnp.maximum(i * (tm // sub) - 1, 0), 0)),
            pl.BlockSpec((1, D), lambda i: (0, 0)),
            pl.BlockSpec((1, 1, D), lambda i: (i // per, 0, 0)),
            pl.BlockSpec((1, 1, D), lambda i: (i // per, 0, 0)),
            pl.BlockSpec((n_mix, D), lambda i: (0, 0)),
        ],
        out_specs=[pl.BlockSpec((tm, D), lambda i: (i, 0))] * n_mix,
        compiler_params=_cparams(("arbitrary",)),
        name="premix",
    )(x, x, g.reshape(1, D), sh, sc, mu)


def _mm_kernel(epilogue, nk, a_ref, w_ref, *rest):
    if epilogue == "resid":
        x_ref, gt_ref, o_ref, acc_ref = rest
    else:
        o_ref, acc_ref = rest
    k = pl.program_id(2)
    part = jnp.dot(a_ref[...], w_ref[...], preferred_element_type=F32)

    def finish(acc):
        if epilogue == "resid":
            o_ref[...] = x_ref[...] + gt_ref[0] * acc
        elif epilogue == "relu2":
            o_ref[...] = jnp.square(jnp.maximum(acc, 0.0)).astype(o_ref.dtype)
        else:
            o_ref[...] = acc.astype(o_ref.dtype)

    if nk == 1:
        finish(part)
    else:
        @pl.when(k == 0)
        def _():
            acc_ref[...] = part

        @pl.when(jnp.logical_and(k > 0, k < nk - 1))
        def _():
            acc_ref[...] += part

        @pl.when(k == nk - 1)
        def _():
            finish(acc_ref[...] + part)


def matmul(a, w, *, epilogue="none", out_dtype=F32, resid=None, gate=None,
           rows_per_batch=None, tm=1024, tn=1024, tk=2048):
    M, K = a.shape
    N = w.shape[1]
    tk = min(tk, K)
    tn = min(tn, N)
    nk = K // tk
    in_specs = [
        pl.BlockSpec((tm, tk), lambda i, j, k: (i, k)),
        pl.BlockSpec((tk, tn), lambda i, j, k: (k, j)),
    ]
    args = [a, w]
    if epilogue == "resid":
        per = rows_per_batch // tm
        in_specs += [
            pl.BlockSpec((tm, tn), lambda i, j, k: (i, j)),
            pl.BlockSpec((1, 1, tn), lambda i, j, k: (i // per, 0, j)),
        ]
        args += [resid, gate]
    acc_shape = (tm, tn) if nk > 1 else (8, LANES)
    return pl.pallas_call(
        functools.partial(_mm_kernel, epilogue, nk),
        out_shape=jax.ShapeDtypeStruct((M, N), out_dtype),
        grid=(M // tm, N // tn, nk),
        in_specs=in_specs,
        out_specs=pl.BlockSpec((tm, tn), lambda i, j, k: (i, j)),
        scratch_shapes=[pltpu.VMEM(acc_shape, F32)],
        compiler_params=_cparams(("arbitrary", "arbitrary", "arbitrary")),
        name="mm_" + epilogue,
    )(*args)


def _lora_kernel(act, a_ref, w1_ref, w2_ref, b_ref, o_ref):
    t = jnp.dot(a_ref[...], w1_ref[...], preferred_element_type=F32)
    if act == "tanh":
        t = jnp.tanh(t)
    elif act == "sigmoid":
        t = jax.nn.sigmoid(t)
    o_ref[...] = jnp.dot(t.astype(BF16), w2_ref[...], preferred_element_type=F32) + b_ref[...]


def lora(a, w1, w2, bias, act, tm=512):
    M, D = a.shape
    R = w1.shape[1]
    N = w2.shape[1]
    return pl.pallas_call(
        functools.partial(_lora_kernel, act),
        out_shape=jax.ShapeDtypeStruct((M, N), F32),
        grid=(M // tm,),
        in_specs=[
            pl.BlockSpec((tm, D), lambda i: (i, 0)),
            pl.BlockSpec((D, R), lambda i: (0, 0)),
            pl.BlockSpec((R, N), lambda i: (0, 0)),
            pl.BlockSpec((1, N), lambda i: (0, 0)),
        ],
        out_specs=pl.BlockSpec((tm, N), lambda i: (i, 0)),
        compiler_params=_cparams(("arbitrary",)),
        name="lora_" + act,
    )(a, w1, w2, bias.reshape(1, N))


def _pad_rank(w1, w2):
    r = w1.shape[1]
    rp = -(-r // LANES) * LANES
    if rp != r:
        w1 = jnp.pad(w1, ((0, 0), (0, rp - r)))
        w2 = jnp.pad(w2, ((0, rp - r), (0, 0)))
    return w1.astype(BF16), w2.astype(BF16)


def _mulcast_kernel(y_ref, g_ref, o_ref):
    o_ref[...] = (y_ref[...] * g_ref[...]).astype(o_ref.dtype)


def mulcast(y, g, tm=512):
    M, D = y.shape
    spec = pl.BlockSpec((tm, D), lambda i: (i, 0))
    return pl.pallas_call(
        _mulcast_kernel,
        out_shape=jax.ShapeDtypeStruct((M, D), BF16),
        grid=(M // tm,),
        in_specs=[spec, spec],
        out_specs=spec,
        compiler_params=_cparams(("arbitrary",)),
        name="mulcast",
    )(y, g)


def _scan_kernel(r_ref, wl_ref, k_ref, v_ref, al_ref, kk_ref, ka_ref, rk_ref,
                 lnw_ref, lnb_ref, o_ref, s_ref, dec_ref, kap_ref, nka_ref, km_ref):
    tb, dh, nl = r_ref.shape

    @pl.when(pl.program_id(0) == 0)
    def _():
        s_ref[...] = jnp.zeros_like(s_ref)

    z = -wl_ref[...]
    softplus = jnp.maximum(z, 0.0) + jnp.log1p(jnp.exp(-jnp.abs(z)))
    dec_ref[...] = jnp.exp(-jnp.exp(-softplus - 0.5))
    a = jax.nn.sigmoid(al_ref[...])
    k = k_ref[...]
    kk = k * kk_ref[...][None]
    nrm = jnp.sqrt(jnp.sum(kk * kk, axis=1, keepdims=True))
    kap = kk / jnp.maximum(nrm, 1e-12)
    kap_ref[...] = kap
    nka_ref[...] = -(kap * a)
    km_ref[...] = k * (1 + (a - 1) * ka_ref[...][None])

    lnw = lnw_ref[...]
    lnb = lnb_ref[...]
    rk = rk_ref[...]

    def row(ref, t, c):
        return jnp.broadcast_to(ref[t, pl.ds(c, 1), :], (dh, nl))

    def step(t, carry):
        acc0 = jnp.zeros((dh, nl), F32)
        acc1 = jnp.zeros((dh, nl), F32)
        for c in range(0, dh, 2):
            acc0 = acc0 + s_ref[c] * row(kap_ref, t, c)
            acc1 = acc1 + s_ref[c + 1] * row(kap_ref, t, c + 1)
        sa = acc0 + acc1
        v_t = v_ref[t]
        y0 = jnp.zeros((dh, nl), F32)
        y1 = jnp.zeros((dh, nl), F32)
        for c in range(dh):
            s_new = (s_ref[c] * row(dec_ref, t, c) + sa * row(nka_ref, t, c)
                     + v_t * row(km_ref, t, c))
            s_ref[c] = s_new
            if c % 2 == 0:
                y0 = y0 + s_new * row(r_ref, t, c)
            else:
                y1 = y1 + s_new * row(r_ref, t, c)
        y = y0 + y1
        mean = jnp.mean(y, axis=0, keepdims=True)
        d = y - mean
        var = jnp.mean(d * d, axis=0, keepdims=True)
        yn = d * lax.rsqrt(var + GN_EPS) * lnw + lnb
        bonus = jnp.sum(r_ref[t] * km_ref[t] * rk, axis=0, keepdims=True)
        o_ref[t] = yn + bonus * v_t
        return carry

    lax.fori_loop(0, tb, step, 0)


def rwkv_scan(r, wl, k, v, al, k_k, k_a, r_k, ln_w, ln_b):
    T, dh, nl = r.shape
    tb = SCAN_TB
    seq = pl.BlockSpec((tb, dh, nl), lambda i: (i, 0, 0))
    par = pl.BlockSpec((dh, nl), lambda i: (0, 0))
    return pl.pallas_call(
        _scan_kernel,
        out_shape=jax.ShapeDtypeStruct((T, dh, nl), F32),
        grid=(T // tb,),
        in_specs=[seq] * 5 + [par] * 5,
        out_specs=seq,
        scratch_shapes=[pltpu.VMEM((dh, dh, nl), F32)] + [pltpu.VMEM((tb, dh, nl), F32)] * 4,
        compiler_params=_cparams(("arbitrary",)),
        name="rwkv_scan",
    )(r, wl, k, v, al, k_k, k_a, r_k, ln_w, ln_b)


def _bias_kernel(g_ref, o_ref):
    nq, nw = o_ref.shape[2], o_ref.shape[3]
    ng = g_ref.shape[2]
    row = lax.broadcasted_iota(jnp.int32, (nq, ng), 0)
    x = jnp.broadcast_to(g_ref[0], (nq, ng))
    shift = 1
    while shift < nq:
        x = jnp.where((row & shift) != 0, pltpu.roll(x, shift, axis=1), x)
        shift *= 2
    qi = lax.broadcasted_iota(jnp.int32, (nq, nw), 0) // CHUNK
    kj = lax.broadcasted_iota(jnp.int32, (nq, nw), 1) // CHUNK
    band = jnp.logical_and(kj >= qi, kj <= qi + N_LEFT_CHUNKS)
    o_ref[0, 0] = jnp.where(band, x[:, :nw], MASK_VALUE)


def band_bias(rel_bias):
    H = rel_bias.shape[0]
    ng = -(-(ATTN_W + ATTN_Q) // LANES) * LANES
    far = rel_bias[:, 2 * REL_CLIP:]
    near = rel_bias[:, 2 * REL_CLIP - 1:REL_CLIP + ATTN_PAD - ATTN_W:-1]
    n_far = ATTN_W - near.shape[1]
    g = jnp.concatenate(
        [jnp.broadcast_to(far, (H, n_far)), near, jnp.broadcast_to(far, (H, ng - ATTN_W))], axis=1)
    return pl.pallas_call(
        _bias_kernel,
        out_shape=jax.ShapeDtypeStruct((H // 2, 2, ATTN_Q, ATTN_W), F32),
        grid=(H // 2, 2),
        in_specs=[pl.BlockSpec((1, 1, ng), lambda p, h: (2 * p + h, 0, 0))],
        out_specs=pl.BlockSpec((1, 1, ATTN_Q, ATTN_W), lambda p, h: (p, h, 0, 0)),
        compiler_params=_cparams(("arbitrary", "arbitrary")),
        name="band_bias",
    )(g.reshape(H, 1, ng))


def _attn_kernel(q_ref, k_ref, v_ref, b_ref, o_ref):
    T = q_ref.shape[1]
    scale = HEAD_DIM ** -0.5
    lane = lax.broadcasted_iota(jnp.int32, (ATTN_Q, LANES), 1)
    kpos = lax.broadcasted_iota(jnp.int32, (ATTN_Q, ATTN_W), 1)
    head_mask = [(lane < HEAD_DIM).astype(F32).astype(BF16),
                 (lane >= HEAD_DIM).astype(F32).astype(BF16)]

    def step(n, carry):
        start = pl.multiple_of(n * ATTN_Q, ATTN_Q)
        q2 = q_ref[0, pl.ds(start, ATTN_Q), :]
        kw = k_ref[0, pl.ds(start, ATTN_W), :]
        vw = v_ref[0, pl.ds(start, ATTN_W), :]
        valid = kpos + start >= ATTN_PAD
        outs = []
        for h in range(2):
            qh = q2 * head_mask[h]
            s = lax.dot_general(qh, kw, (((1,), (1,)), ((), ())),
                                preferred_element_type=F32)
            s = s * scale + b_ref[0, h]
            s = jnp.where(valid, s, MASK_VALUE)
            m = jnp.max(s, axis=-1, keepdims=True)
            p = jnp.exp(s - m)
            l = jnp.sum(p, axis=-1, keepdims=True)
            o = jnp.dot(p.astype(BF16), vw, preferred_element_type=F32)
            outs.append(o / l)
        o_ref[0, pl.ds(start, ATTN_Q), :] = jnp.where(
            lane < HEAD_DIM, outs[0], outs[1]).astype(o_ref.dtype)
        return carry

    lax.fori_loop(0, T // ATTN_Q, step, 0)


def band_attention(q, kvp, bias):
    B, T, D = q.shape
    Tp = kvp.shape[1]
    n_pairs = D // LANES
    return pl.pallas_call(
        _attn_kernel,
        out_shape=jax.ShapeDtypeStruct((B, T, D), BF16),
        grid=(B, n_pairs),
        in_specs=[
            pl.BlockSpec((1, T, LANES), lambda b, p: (b, 0, p)),
            pl.BlockSpec((1, Tp, LANES), lambda b, p: (b, 0, p)),
            pl.BlockSpec((1, Tp, LANES), lambda b, p: (b, 0, p + n_pairs)),
            pl.BlockSpec((1, 2, ATTN_Q, ATTN_W), lambda b, p: (p, 0, 0, 0)),
        ],
        out_specs=pl.BlockSpec((1, T, LANES), lambda b, p: (b, 0, p)),
        compiler_params=_cparams(("arbitrary", "arbitrary")),
        name="band_attention",
    )(q, kvp, kvp, bias)


def _to_head_lanes(a, B, T, H):
    return jnp.transpose(a.reshape(B, T, H, HEAD_DIM), (1, 3, 0, 2)).reshape(T, HEAD_DIM, B * H)


def _from_head_lanes(a, B, T, H):
    return jnp.transpose(a.reshape(T, HEAD_DIM, B, H), (2, 0, 3, 1)).reshape(B * T, H * HEAD_DIM)


def _param_head_lanes(p, B, H):
    t = jnp.transpose(p.reshape(H, HEAD_DIM).astype(F32))
    return jnp.tile(t, (1, B))


def kernel(x, c, w_ada, b_ada, g_mix, g_mlp, w_up, w_down, rwkv_mu, rwkv_w_r, rwkv_w_k, rwkv_w_v, rwkv_w_o, rwkv_w0, rwkv_w1, rwkv_w2, rwkv_a0, rwkv_a1, rwkv_a2, rwkv_g1, rwkv_g2, rwkv_k_k, rwkv_k_a, rwkv_r_k, rwkv_ln_w, rwkv_ln_b, attn_w_q, attn_w_o, attn_rel_bias, w_ada_kv, b_ada_kv, g_kv, w_k_shared, w_v_shared, g_final):
    B, T, D = x.shape
    H = D // HEAD_DIM
    M = B * T
    depth = w_ada.shape[0]
    n_a = rwkv_mu.shape[0]
    assert T % ATTN_Q == 0 and T % SCAN_TB == 0 and B * H == LANES

    xf = x.reshape(M, D)
    c_pad = jnp.pad(c, ((0, 16 - B), (0, 0))).astype(BF16)
    mods = ada_mod(c_pad, w_ada, b_ada)[:, :B]
    mod_kv = ada_mod(c_pad, w_ada_kv[None], b_ada_kv[None])[0, :B]

    def split(m, n):
        return [p.reshape(B, 1, D) for p in jnp.split(m, n, axis=-1)]

    kvp = None
    for layer in range(depth):
        sh1, sc1, gt1, sh2, sc2, gt2 = split(mods[layer], 6)
        if layer == n_a:
            sh_kv, sc_kv = split(mod_kv, 2)
            h_kv = norm_mod(xf, g_kv, sh_kv, sc_kv, T)
            w_kv = jnp.concatenate([w_k_shared, w_v_shared], axis=1).astype(BF16)
            kv = matmul(h_kv, w_kv, out_dtype=BF16).reshape(B, T, 2 * D)
            kvp = jnp.pad(kv, ((0, 0), (ATTN_PAD, 0), (0, 0)))
        if layer < n_a:
            i = layer
            xr, xw, xk, xv, xa, xg = premix(xf, g_mix[layer], sh1, sc1, rwkv_mu[i], T)
            r = matmul(xr, rwkv_w_r[i].astype(BF16))
            k = matmul(xk, rwkv_w_k[i].astype(BF16))
            v = matmul(xv, rwkv_w_v[i].astype(BF16))
            wl = lora(xw, *_pad_rank(rwkv_w1[i], rwkv_w2[i]), rwkv_w0[i], "tanh")
            al = lora(xa, *_pad_rank(rwkv_a1[i], rwkv_a2[i]), rwkv_a0[i], "none")
            g = lora(xg, *_pad_rank(rwkv_g1[i], rwkv_g2[i]), jnp.zeros((D,), F32), "sigmoid")
            hl = functools.partial(_to_head_lanes, B=B, T=T, H=H)
            pl_ = functools.partial(_param_head_lanes, B=B, H=H)
            y = rwkv_scan(hl(r), hl(wl), hl(k), hl(v), hl(al),
                          pl_(rwkv_k_k[i]), pl_(rwkv_k_a[i]), pl_(rwkv_r_k[i].reshape(D)),
                          pl_(rwkv_ln_w[i]), pl_(rwkv_ln_b[i]))
            y = _from_head_lanes(y, B, T, H)
            yg = mulcast(y, g)
            w_o = rwkv_w_o[i]
        else:
            i = layer - n_a
            h = norm_mod(xf, g_mix[layer], sh1, sc1, T)
            q = matmul(h, attn_w_q[i].astype(BF16), out_dtype=BF16).reshape(B, T, D)
            bias = band_bias(attn_rel_bias[i].astype(F32))
            yg = band_attention(q, kvp, bias).reshape(M, D)
            w_o = attn_w_o[i]
        xf = matmul(yg, w_o.astype(BF16), epilogue="resid", resid=xf, gate=gt1,
                    rows_per_batch=T)
        h = norm_mod(xf, g_mlp[layer], sh2, sc2, T)
        hid = matmul(h, w_up[layer].astype(BF16), epilogue="relu2", out_dtype=BF16)
        xf = matmul(hid, w_down[layer].astype(BF16), epilogue="resid", resid=xf, gate=gt2,
                    rows_per_batch=T)
    return final_norm(xf, g_final).reshape(B, T, D)
```

```python
import functools

import jax
import jax.numpy as jnp
from jax import lax
from jax.experimental import pallas as pl
from jax.experimental.pallas import tpu as pltpu

F32 = jnp.float32
BF16 = jnp.bfloat16

CHUNK = 64
N_LEFT_CHUNKS = 8
HEAD_DIM = 64
REL_CLIP = 256
RMS_EPS = 1e-6
GN_EPS = 64e-5
MASK_VALUE = -1e30

LANES = 128
ATTN_Q = 2 * CHUNK
ATTN_PAD = N_LEFT_CHUNKS * CHUNK
ATTN_W = ATTN_PAD + ATTN_Q
SCAN_TB = 32
SCAN_KU = 16
VMEM_LIMIT = 48 * 1024 * 1024


def _cparams(sem):
    return pltpu.CompilerParams(dimension_semantics=sem, vmem_limit_bytes=VMEM_LIMIT)


def _ada_kernel(c_ref, w_ref, b_ref, o_ref):
    w = w_ref[0].astype(BF16)
    o_ref[0] = jnp.dot(c_ref[...], w, preferred_element_type=F32) + b_ref[0]


def ada_mod(c_pad, w, b, tn=1024):
    L, D, N = w.shape
    R = c_pad.shape[0]
    return pl.pallas_call(
        _ada_kernel,
        out_shape=jax.ShapeDtypeStruct((L, R, N), F32),
        grid=(L, N // tn),
        in_specs=[
            pl.BlockSpec((R, D), lambda l, j: (0, 0)),
            pl.BlockSpec((1, D, tn), lambda l, j: (l, 0, j)),
            pl.BlockSpec((1, 1, tn), lambda l, j: (l, 0, j)),
        ],
        out_specs=pl.BlockSpec((1, R, tn), lambda l, j: (l, 0, j)),
        compiler_params=_cparams(("arbitrary", "arbitrary")),
        name="ada_mod",
    )(c_pad, w, b.reshape(L, 1, N))


def _rms(x, g):
    ms = jnp.mean(x * x, axis=-1, keepdims=True)
    return x * lax.rsqrt(ms + RMS_EPS) * g


def _norm_mod_kernel(x_ref, g_ref, sh_ref, sc_ref, o_ref):
    y = _rms(x_ref[...], g_ref[...])
    o_ref[...] = (y * (1 + sc_ref[0]) + sh_ref[0]).astype(o_ref.dtype)


def norm_mod(x, g, sh, sc, rows_per_batch, tm=512):
    M, D = x.shape
    per = rows_per_batch // tm
    return pl.pallas_call(
        _norm_mod_kernel,
        out_shape=jax.ShapeDtypeStruct((M, D), BF16),
        grid=(M // tm,),
        in_specs=[
            pl.BlockSpec((tm, D), lambda i: (i, 0)),
            pl.BlockSpec((1, D), lambda i: (0, 0)),
            pl.BlockSpec((1, 1, D), lambda i: (i // per, 0, 0)),
            pl.BlockSpec((1, 1, D), lambda i: (i // per, 0, 0)),
        ],
        out_specs=pl.BlockSpec((tm, D), lambda i: (i, 0)),
        compiler_params=_cparams(("arbitrary",)),
        name="norm_mod",
    )(x, g.reshape(1, D), sh, sc)


def _final_norm_kernel(x_ref, g_ref, o_ref):
    o_ref[...] = _rms(x_ref[...], g_ref[...])


def final_norm(x, g, tm=512):
    M, D = x.shape
    return pl.pallas_call(
        _final_norm_kernel,
        out_shape=jax.ShapeDtypeStruct((M, D), F32),
        grid=(M // tm,),
        in_specs=[
            pl.BlockSpec((tm, D), lambda i: (i, 0)),
            pl.BlockSpec((1, D), lambda i: (0, 0)),
        ],
        out_specs=pl.BlockSpec((tm, D), lambda i: (i, 0)),
        compiler_params=_cparams(("arbitrary",)),
        name="final_norm",
    )(x, g.reshape(1, D))


def _premix_kernel(per, x_ref, xp_ref, g_ref, sh_ref, sc_ref, mu_ref, *o_refs):
    g = g_ref[...]
    sc = 1 + sc_ref[0]
    sh = sh_ref[0]
    h = _rms(x_ref[...], g) * sc + sh
    hp = (_rms(xp_ref[...], g) * sc + sh)[7:8, :]
    first_tile = (pl.program_id(0) % per) == 0
    hp = jnp.where(first_tile, 0.0, hp)
    row = lax.broadcasted_iota(jnp.int32, h.shape, 0)
    hs = jnp.where(row == 0, hp, pltpu.roll(h, 1, axis=0))
    xx = hs - h
    for j, o_ref in enumerate(o_refs):
        o_ref[...] = (h + xx * mu_ref[j:j + 1, :]).astype(o_ref.dtype)


def premix(x, g, sh, sc, mu, rows_per_batch, tm=256):
    M, D = x.shape
    per = rows_per_batch // tm
    sub = 8
    n_mix = mu.shape[0]
    return pl.pallas_call(
        functools.partial(_premix_kernel, per),
        out_shape=[jax.ShapeDtypeStruct((M, D), BF16)] * n_mix,
        grid=(M // tm,),
        in_specs=[
            pl.BlockSpec((tm, D), lambda i: (i, 0)),
            pl.BlockSpec((sub, D), lambda i: (jnp.maximum(i * (tm // sub) - 1, 0), 0)),
            pl.BlockSpec((1, D), lambda i: (0, 0)),
            pl.BlockSpec((1, 1, D), lambda i: (i // per, 0, 0)),
            pl.BlockSpec((1, 1, D), lambda i: (i // per, 0, 0)),
            pl.BlockSpec((n_mix, D), lambda i: (0, 0)),
        ],
        out_specs=[pl.BlockSpec((tm, D), lambda i: (i, 0))] * n_mix,
        compiler_params=_cparams(("arbitrary",)),
        name="premix",
    )(x, x, g.reshape(1, D), sh, sc, mu)


def _mm_kernel(epilogue, nk, a_ref, w_ref, *rest):
    if epilogue == "resid":
        x_ref, gt_ref, o_ref, acc_ref = rest
    else:
        o_ref, acc_ref = rest
    k = pl.program_id(2)
    part = jnp.dot(a_ref[...], w_ref[...], preferred_element_type=F32)

    def finish(acc):
        if epilogue == "resid":
            o_ref[...] = x_ref[...] + gt_ref[0] * acc
        elif epilogue == "relu2":
            o_ref[...] = jnp.square(jnp.maximum(acc, 0.0)).astype(o_ref.dtype)
        else:
            o_ref[...] = acc.astype(o_ref.dtype)

    if nk == 1:
        finish(part)
    else:
        @pl.when(k == 0)
        def _():
            acc_ref[...] = part

        @pl.when(jnp.logical_and(k > 0, k < nk - 1))
        def _():
            acc_ref[...] += part

        @pl.when(k == nk - 1)
        def _():
            finish(acc_ref[...] + part)


def matmul(a, w, *, epilogue="none", out_dtype=F32, resid=None, gate=None,
           rows_per_batch=None, tm=1024, tn=1024, tk=2048):
    M, K = a.shape
    N = w.shape[1]
    tk = min(tk, K)
    tn = min(tn, N)
    nk = K // tk
    in_specs = [
        pl.BlockSpec((tm, tk), lambda i, j, k: (i, k)),
        pl.BlockSpec((tk, tn), lambda i, j, k: (k, j)),
    ]
    args = [a, w]
    if epilogue == "resid":
        per = rows_per_batch // tm
        in_specs += [
            pl.BlockSpec((tm, tn), lambda i, j, k: (i, j)),
            pl.BlockSpec((1, 1, tn), lambda i, j, k: (i // per, 0, j)),
        ]
        args += [resid, gate]
    acc_shape = (tm, tn) if nk > 1 else (8, LANES)
    return pl.pallas_call(
        functools.partial(_mm_kernel, epilogue, nk),
        out_shape=jax.ShapeDtypeStruct((M, N), out_dtype),
        grid=(M // tm, N // tn, nk),
        in_specs=in_specs,
        out_specs=pl.BlockSpec((tm, tn), lambda i, j, k: (i, j)),
        scratch_shapes=[pltpu.VMEM(acc_shape, F32)],
        compiler_params=_cparams(("arbitrary", "arbitrary", "arbitrary")),
        name="mm_" + epilogue,
    )(*args)


def _lora_kernel(act, a_ref, w1_ref, w2_ref, b_ref, o_ref):
    t = jnp.dot(a_ref[...], w1_ref[...], preferred_element_type=F32)
    if act == "tanh":
        t = jnp.tanh(t)
    elif act == "sigmoid":
        t = jax.nn.sigmoid(t)
    o_ref[...] = jnp.dot(t.astype(BF16), w2_ref[...], preferred_element_type=F32) + b_ref[...]


def lora(a, w1, w2, bias, act, tm=512):
    M, D = a.shape
    R = w1.shape[1]
    N = w2.shape[1]
    return pl.pallas_call(
        functools.partial(_lora_kernel, act),
        out_shape=jax.ShapeDtypeStruct((M, N), F32),
        grid=(M // tm,),
        in_specs=[
            pl.BlockSpec((tm, D), lambda i: (i, 0)),
            pl.BlockSpec((D, R), lambda i: (0, 0)),
            pl.BlockSpec((R, N), lambda i: (0, 0)),
            pl.BlockSpec((1, N), lambda i: (0, 0)),
        ],
        out_specs=pl.BlockSpec((tm, N), lambda i: (i, 0)),
        compiler_params=_cparams(("arbitrary",)),
        name="lora_" + act,
    )(a, w1, w2, bias.reshape(1, N))


def _pad_rank(w1, w2):
    r = w1.shape[1]
    rp = -(-r // LANES) * LANES
    if rp != r:
        w1 = jnp.pad(w1, ((0, 0), (0, rp - r)))
        w2 = jnp.pad(w2, ((0, rp - r), (0, 0)))
    return w1.astype(BF16), w2.astype(BF16)


def _mulcast_kernel(y_ref, g_ref, o_ref):
    o_ref[...] = (y_ref[...] * g_ref[...]).astype(o_ref.dtype)


def mulcast(y, g, tm=512):
    M, D = y.shape
    spec = pl.BlockSpec((tm, D), lambda i: (i, 0))
    return pl.pallas_call(
        _mulcast_kernel,
        out_shape=jax.ShapeDtypeStruct((M, D), BF16),
        grid=(M // tm,),
        in_specs=[spec, spec],
        out_specs=spec,
        compiler_params=_cparams(("arbitrary",)),
        name="mulcast",
    )(y, g)


def _group_rows_to_head_lanes(x_ref, dst_ref):
    nb, tb, ng, nl = x_ref.shape
    width = nl // nb
    grp = lax.broadcasted_iota(jnp.int32, (tb * ng, nl), 1) // width
    xs = [x_ref[b].reshape(tb * ng, nl) for b in range(nb)]
    for c in range(nb):
        acc = None
        for b in range(nb):
            shift = ((b - c) * width) % nl
            piece = pltpu.roll(xs[b], shift, axis=1) if shift else xs[b]
            acc = piece if acc is None else jnp.where(grp == b, piece, acc)
        dst_ref[:, c * ng:(c + 1) * ng, :] = acc.reshape(tb, ng, nl)


def _head_lanes_to_group_rows(src_ref, o_ref):
    nb, tb, ng, nl = o_ref.shape
    width = nl // nb
    grp = lax.broadcasted_iota(jnp.int32, (tb * ng, nl), 1) // width
    ys = [src_ref[:, c * ng:(c + 1) * ng, :].reshape(tb * ng, nl) for c in range(nb)]
    for b in range(nb):
        acc = None
        for c in range(nb):
            shift = ((c - b) * width) % nl
            piece = pltpu.roll(ys[c], shift, axis=1) if shift else ys[c]
            acc = piece if acc is None else jnp.where(grp == c, piece, acc)
        o_ref[b] = acc.reshape(tb, ng, nl)


def _scan_kernel(r_ref, wl_ref, k_ref, v_ref, al_ref, kk_ref, ka_ref, rk_ref,
                 lnw_ref, lnb_ref, o_ref,
                 s_ref, r_s, v_s, dec_s, kap_s, nka_s, km_s, y_s):
    tb, dh, nl = r_s.shape
    half = dh // 2

    @pl.when(pl.program_id(0) == 0)
    def _():
        s_ref[...] = jnp.zeros_like(s_ref)

    _group_rows_to_head_lanes(r_ref, r_s)
    _group_rows_to_head_lanes(v_ref, v_s)
    _group_rows_to_head_lanes(wl_ref, dec_s)
    _group_rows_to_head_lanes(al_ref, nka_s)
    _group_rows_to_head_lanes(k_ref, km_s)

    z = -dec_s[...]
    softplus = jnp.maximum(z, 0.0) + jnp.log1p(jnp.exp(-jnp.abs(z)))
    dec_s[...] = jnp.exp(-jnp.exp(-softplus - 0.5))
    a = jax.nn.sigmoid(nka_s[...])
    k = km_s[...]
    kk = k * kk_ref[...][None]
    nrm = jnp.sqrt(jnp.sum(kk * kk, axis=1, keepdims=True))
    kap = kk / jnp.maximum(nrm, 1e-12)
    kap_s[...] = kap
    nka_s[...] = -(kap * a)
    km_s[...] = k * (1 + (a - 1) * ka_ref[...][None])

    def row(ref, t, c):
        return jnp.broadcast_to(ref[t, pl.ds(c, 1), :], (half, nl))

    zeros = jnp.zeros((half, nl), F32)

    def first_sa(hf):
        def body(g, acc):
            for j in range(SCAN_KU):
                c = g * SCAN_KU + j
                acc = acc + s_ref[hf, c] * row(kap_s, 0, c)
            return acc
        return lax.fori_loop(0, dh // SCAN_KU, body, zeros)

    def run_half(hf, t, t_next, sa):
        v_t = v_s[t, hf * half:(hf + 1) * half, :]

        def body(g, carry):
            y, sa_next = carry
            for j in range(SCAN_KU):
                c = g * SCAN_KU + j
                s_new = (s_ref[hf, c] * row(dec_s, t, c) + sa * row(nka_s, t, c)
                         + v_t * row(km_s, t, c))
                s_ref[hf, c] = s_new
                y = y + s_new * row(r_s, t, c)
                sa_next = sa_next + s_new * row(kap_s, t_next, c)
            return y, sa_next

        y, sa_next = lax.fori_loop(0, dh // SCAN_KU, body, (zeros, zeros))
        y_s[t, hf * half:(hf + 1) * half, :] = y
        return sa_next

    def step(t, sa):
        t_next = jnp.minimum(t + 1, tb - 1)
        return run_half(0, t, t_next, sa[0]), run_half(1, t, t_next, sa[1])

    lax.fori_loop(0, tb, step, (first_sa(0), first_sa(1)))

    y = y_s[...]
    mean = jnp.mean(y, axis=1, keepdims=True)
    d = y - mean
    var = jnp.mean(d * d, axis=1, keepdims=True)
    yn = d * lax.rsqrt(var + GN_EPS) * lnw_ref[...][None] + lnb_ref[...][None]
    bonus = jnp.sum(r_s[...] * km_s[...] * rk_ref[...][None], axis=1, keepdims=True)
    y_s[...] = yn + bonus * v_s[...]
    _head_lanes_to_group_rows(y_s, o_ref)


def rwkv_scan(r, wl, k, v, al, k_k, k_a, r_k, ln_w, ln_b):
    B, T, ng, nl = r.shape
    dh = k_k.shape[0]
    tb = SCAN_TB
    seq = pl.BlockSpec((B, tb, ng, nl), lambda i: (0, i, 0, 0))
    par = pl.BlockSpec((dh, nl), lambda i: (0, 0))
    return pl.pallas_call(
        _scan_kernel,
        out_shape=jax.ShapeDtypeStruct((B, T, ng, nl), F32),
        grid=(T // tb,),
        in_specs=[seq] * 5 + [par] * 5,
        out_specs=seq,
        scratch_shapes=([pltpu.VMEM((2, dh, dh // 2, nl), F32)]
                        + [pltpu.VMEM((tb, dh, nl), F32)] * 7),
        compiler_params=_cparams(("arbitrary",)),
        name="rwkv_scan",
    )(r, wl, k, v, al, k_k, k_a, r_k, ln_w, ln_b)


def _bias_kernel(g_ref, o_ref):
    nq, nw = o_ref.shape[1], o_ref.shape[2]
    ng = g_ref.shape[2]
    row = lax.broadcasted_iota(jnp.int32, (nq, ng), 0)
    x = jnp.broadcast_to(g_ref[0], (nq, ng))
    shift = 1
    while shift < nq:
        x = jnp.where((row & shift) != 0, pltpu.roll(x, shift, axis=1), x)
        shift *= 2
    qi = lax.broadcasted_iota(jnp.int32, (nq, nw), 0) // CHUNK
    kj = lax.broadcasted_iota(jnp.int32, (nq, nw), 1) // CHUNK
    band = jnp.logical_and(kj >= qi, kj <= qi + N_LEFT_CHUNKS)
    o_ref[0] = jnp.where(band, x[:, :nw], MASK_VALUE)


def band_bias(rel_bias):
    H = rel_bias.shape[0]
    ng = -(-(ATTN_W + ATTN_Q) // LANES) * LANES
    far = rel_bias[:, 2 * REL_CLIP:]
    near = rel_bias[:, 2 * REL_CLIP - 1:REL_CLIP + ATTN_PAD - ATTN_W:-1]
    n_far = ATTN_W - near.shape[1]
    g = jnp.concatenate(
        [jnp.broadcast_to(far, (H, n_far)), near, jnp.broadcast_to(far, (H, ng - ATTN_W))], axis=1)
    return pl.pallas_call(
        _bias_kernel,
        out_shape=jax.ShapeDtypeStruct((H, ATTN_Q, ATTN_W), F32),
        grid=(H,),
        in_specs=[pl.BlockSpec((1, 1, ng), lambda h: (h, 0, 0))],
        out_specs=pl.BlockSpec((1, ATTN_Q, ATTN_W), lambda h: (h, 0, 0)),
        compiler_params=_cparams(("arbitrary",)),
        name="band_bias",
    )(g.reshape(H, 1, ng))


def _attn_kernel(q_ref, k_ref, v_ref, b_ref, o_ref):
    T = q_ref.shape[1]
    n_steps = T // ATTN_Q
    n_masked = -(-ATTN_PAD // ATTN_Q)
    lane = lax.broadcasted_iota(jnp.int32, (ATTN_Q, LANES), 1)
    scale = HEAD_DIM ** -0.5
    head_scale = [jnp.where(lane < HEAD_DIM, scale, 0.0).astype(BF16),
                  jnp.where(lane >= HEAD_DIM, scale, 0.0).astype(BF16)]
    kpos = lax.broadcasted_iota(jnp.int32, (2 * ATTN_Q, ATTN_W), 1)

    def step(n, masked):
        start = pl.multiple_of(n * ATTN_Q, ATTN_Q)
        q2 = q_ref[0, pl.ds(start, ATTN_Q), :]
        kw = k_ref[0, pl.ds(start, ATTN_W), :]
        vw = v_ref[0, pl.ds(start, ATTN_W), :]
        qs = jnp.concatenate([q2 * head_scale[0], q2 * head_scale[1]], axis=0)
        s = lax.dot_general(qs, kw, (((1,), (1,)), ((), ())), preferred_element_type=F32)
        s = s + b_ref[...].reshape(2 * ATTN_Q, ATTN_W)
        if masked:
            s = jnp.where(kpos + start >= ATTN_PAD, s, MASK_VALUE)
        m = jnp.max(s, axis=-1, keepdims=True)
        p = jnp.exp(s - m)
        l = jnp.sum(p, axis=-1, keepdims=True)
        o = jnp.dot(p.astype(BF16), vw, preferred_element_type=F32) / l
        o_ref[0, pl.ds(start, ATTN_Q), :] = jnp.where(
            lane < HEAD_DIM, o[:ATTN_Q], o[ATTN_Q:]).astype(o_ref.dtype)

    def masked_body(n, carry):
        step(n, True)
        return carry

    def body(n, carry):
        step(n, False)
        return carry

    lax.fori_loop(0, n_masked, masked_body, 0)
    lax.fori_loop(n_masked, n_steps, body, 0, unroll=2)


def band_attention(q, kvp, bias):
    B, T, D = q.shape
    Tp = kvp.shape[1]
    n_pairs = D // LANES
    return pl.pallas_call(
        _attn_kernel,
        out_shape=jax.ShapeDtypeStruct((B, T, D), BF16),
        grid=(B, n_pairs),
        in_specs=[
            pl.BlockSpec((1, T, LANES), lambda b, p: (b, 0, p)),
            pl.BlockSpec((1, Tp, LANES), lambda b, p: (b, 0, p)),
            pl.BlockSpec((1, Tp, LANES), lambda b, p: (b, 0, p + n_pairs)),
            pl.BlockSpec((2, ATTN_Q, ATTN_W), lambda b, p: (p, 0, 0)),
        ],
        out_specs=pl.BlockSpec((1, T, LANES), lambda b, p: (b, 0, p)),
        compiler_params=_cparams(("arbitrary", "arbitrary")),
        name="band_attention",
    )(q, kvp, kvp, bias)


def _group_cols(w, H, group):
    lead = w.shape[:-1]
    w = w.reshape(lead + (H, HEAD_DIM // group, group))
    w = jnp.moveaxis(w, -3, -1)
    return w.reshape(lead + (H * HEAD_DIM,))


def _param_head_lanes(p, B, H, group):
    t = jnp.transpose(p.astype(F32).reshape(H, HEAD_DIM // group, group), (2, 1, 0))
    return jnp.tile(t.reshape(HEAD_DIM, H), (1, B))


def kernel(x, c, w_ada, b_ada, g_mix, g_mlp, w_up, w_down, rwkv_mu, rwkv_w_r, rwkv_w_k, rwkv_w_v, rwkv_w_o, rwkv_w0, rwkv_w1, rwkv_w2, rwkv_a0, rwkv_a1, rwkv_a2, rwkv_g1, rwkv_g2, rwkv_k_k, rwkv_k_a, rwkv_r_k, rwkv_ln_w, rwkv_ln_b, attn_w_q, attn_w_o, attn_rel_bias, w_ada_kv, b_ada_kv, g_kv, w_k_shared, w_v_shared, g_final):
    B, T, D = x.shape
    H = D // HEAD_DIM
    M = B * T
    depth = w_ada.shape[0]
    n_a = rwkv_mu.shape[0]
    group = LANES // H
    n_groups = HEAD_DIM // group
    assert T % ATTN_Q == 0 and T % SCAN_TB == 0 and B * H == LANES and group == B

    xf = x.reshape(M, D)
    c_pad = jnp.pad(c, ((0, 16 - B), (0, 0))).astype(BF16)
    mods = ada_mod(c_pad, w_ada, b_ada)[:, :B]
    mod_kv = ada_mod(c_pad, w_ada_kv[None], b_ada_kv[None])[0, :B]

    def split(m, n):
        return [p.reshape(B, 1, D) for p in jnp.split(m, n, axis=-1)]

    kvp = None
    for layer in range(depth):
        sh1, sc1, gt1, sh2, sc2, gt2 = split(mods[layer], 6)
        if layer == n_a:
            sh_kv, sc_kv = split(mod_kv, 2)
            h_kv = norm_mod(xf, g_kv, sh_kv, sc_kv, T)
            w_kv = jnp.concatenate([w_k_shared, w_v_shared], axis=1).astype(BF16)
            kv = matmul(h_kv, w_kv, out_dtype=BF16).reshape(B, T, 2 * D)
            kvp = jnp.pad(kv, ((0, 0), (ATTN_PAD, 0), (0, 0)))
        if layer < n_a:
            i = layer
            gc = functools.partial(_group_cols, H=H, group=group)
            hp = functools.partial(_param_head_lanes, B=B, H=H, group=group)
            xr, xw, xk, xv, xa, xg = premix(xf, g_mix[layer], sh1, sc1, rwkv_mu[i], T)
            r = matmul(xr, gc(rwkv_w_r[i]).astype(BF16))
            k = matmul(xk, gc(rwkv_w_k[i]).astype(BF16))
            v = matmul(xv, gc(rwkv_w_v[i]).astype(BF16))
            wl = lora(xw, *_pad_rank(rwkv_w1[i], gc(rwkv_w2[i])), gc(rwkv_w0[i]), "tanh")
            al = lora(xa, *_pad_rank(rwkv_a1[i], gc(rwkv_a2[i])), gc(rwkv_a0[i]), "none")
            g = lora(xg, *_pad_rank(rwkv_g1[i], gc(rwkv_g2[i])), jnp.zeros((D,), F32), "sigmoid")
            rows = lambda a: a.reshape(B, T, n_groups, LANES)
            y = rwkv_scan(rows(r), rows(wl), rows(k), rows(v), rows(al),
                          hp(rwkv_k_k[i]), hp(rwkv_k_a[i]), hp(rwkv_r_k[i].reshape(D)),
                          hp(rwkv_ln_w[i]), hp(rwkv_ln_b[i]))
            yg = mulcast(y.reshape(M, D), g)
            w_o = gc(rwkv_w_o[i].T).T
        else:
            i = layer - n_a
            h = norm_mod(xf, g_mix[layer], sh1, sc1, T)
            q = matmul(h, attn_w_q[i].astype(BF16), out_dtype=BF16).reshape(B, T, D)
            bias = band_bias(attn_rel_bias[i].astype(F32))
            yg = band_attention(q, kvp, bias).reshape(M, D)
            w_o = attn_w_o[i]
        xf = matmul(yg, w_o.astype(BF16), epilogue="resid", resid=xf, gate=gt1,
                    rows_per_batch=T)
        h = norm_mod(xf, g_mlp[layer], sh2, sc2, T)
        hid = matmul(h, w_up[layer].astype(BF16), epilogue="relu2", out_dtype=BF16)
        xf = matmul(hid, w_down[layer].astype(BF16), epilogue="resid", resid=xf, gate=gt2,
                    rows_per_batch=T)
    return final_norm(xf, g_final).reshape(B, T, D)
```

```python
import functools

import jax
import jax.numpy as jnp
from jax import lax
from jax.experimental import pallas as pl
from jax.experimental.pallas import tpu as pltpu

F32 = jnp.float32
BF16 = jnp.bfloat16

CHUNK = 64
N_LEFT_CHUNKS = 8
HEAD_DIM = 64
REL_CLIP = 256
RMS_EPS = 1e-6
GN_EPS = 64e-5
MASK_VALUE = -1e30

LANES = 128
ATTN_Q = 2 * CHUNK
ATTN_PAD = N_LEFT_CHUNKS * CHUNK
ATTN_W = ATTN_PAD + ATTN_Q
SCAN_TB = 32
SCAN_KU = 16
VMEM_LIMIT = 48 * 1024 * 1024
SQUARE_TILES = dict(tm=512, tn=2048)


def _cparams(sem):
    return pltpu.CompilerParams(dimension_semantics=sem, vmem_limit_bytes=VMEM_LIMIT)


def _ada_kernel(c_ref, w_ref, b_ref, o_ref):
    w = w_ref[0].astype(BF16)
    o_ref[0] = jnp.dot(c_ref[...], w, preferred_element_type=F32) + b_ref[0]


def ada_mod(c_pad, w, b, tn=1024):
    L, D, N = w.shape
    R = c_pad.shape[0]
    return pl.pallas_call(
        _ada_kernel,
        out_shape=jax.ShapeDtypeStruct((L, R, N), F32),
        grid=(L, N // tn),
        in_specs=[
            pl.BlockSpec((R, D), lambda l, j: (0, 0)),
            pl.BlockSpec((1, D, tn), lambda l, j: (l, 0, j)),
            pl.BlockSpec((1, 1, tn), lambda l, j: (l, 0, j)),
        ],
        out_specs=pl.BlockSpec((1, R, tn), lambda l, j: (l, 0, j)),
        compiler_params=_cparams(("arbitrary", "arbitrary")),
        name="ada_mod",
    )(c_pad, w, b.reshape(L, 1, N))


def _rms(x, g):
    ms = jnp.mean(x * x, axis=-1, keepdims=True)
    return x * lax.rsqrt(ms + RMS_EPS) * g


def _norm_mod_kernel(x_ref, g_ref, sh_ref, sc_ref, o_ref):
    y = _rms(x_ref[...], g_ref[...])
    o_ref[...] = (y * (1 + sc_ref[0]) + sh_ref[0]).astype(o_ref.dtype)


def norm_mod(x, g, sh, sc, rows_per_batch, tm=512):
    M, D = x.shape
    per = rows_per_batch // tm
    return pl.pallas_call(
        _norm_mod_kernel,
        out_shape=jax.ShapeDtypeStruct((M, D), BF16),
        grid=(M // tm,),
        in_specs=[
            pl.BlockSpec((tm, D), lambda i: (i, 0)),
            pl.BlockSpec((1, D), lambda i: (0, 0)),
            pl.BlockSpec((1, 1, D), lambda i: (i // per, 0, 0)),
            pl.BlockSpec((1, 1, D), lambda i: (i // per, 0, 0)),
        ],
        out_specs=pl.BlockSpec((tm, D), lambda i: (i, 0)),
        compiler_params=_cparams(("arbitrary",)),
        name="norm_mod",
    )(x, g.reshape(1, D), sh, sc)


def _final_norm_kernel(x_ref, g_ref, o_ref):
    o_ref[...] = _rms(x_ref[...], g_ref[...])


def final_norm(x, g, tm=512):
    M, D = x.shape
    return pl.pallas_call(
        _final_norm_kernel,
        out_shape=jax.ShapeDtypeStruct((M, D), F32),
        grid=(M // tm,),
        in_specs=[
            pl.BlockSpec((tm, D), lambda i: (i, 0)),
            pl.BlockSpec((1, D), lambda i: (0, 0)),
        ],
        out_specs=pl.BlockSpec((tm, D), lambda i: (i, 0)),
        compiler_params=_cparams(("arbitrary",)),
        name="final_norm",
    )(x, g.reshape(1, D))


def _premix_kernel(per, x_ref, xp_ref, g_ref, sh_ref, sc_ref, mu_ref, *o_refs):
    g = g_ref[...]
    sc = 1 + sc_ref[0]
    sh = sh_ref[0]
    h = _rms(x_ref[...], g) * sc + sh
    hp = (_rms(xp_ref[...], g) * sc + sh)[7:8, :]
    first_tile = (pl.program_id(0) % per) == 0
    hp = jnp.where(first_tile, 0.0, hp)
    row = lax.broadcasted_iota(jnp.int32, h.shape, 0)
    hs = jnp.where(row == 0, hp, pltpu.roll(h, 1, axis=0))
    xx = hs - h
    for j, o_ref in enumerate(o_refs):
        o_ref[...] = (h + xx * mu_ref[j:j + 1, :]).astype(o_ref.dtype)


def premix(x, g, sh, sc, mu, rows_per_batch, tm=256):
    M, D = x.shape
    per = rows_per_batch // tm
    sub = 8
    n_mix = mu.shape[0]
    return pl.pallas_call(
        functools.partial(_premix_kernel, per),
        out_shape=[jax.ShapeDtypeStruct((M, D), BF16)] * n_mix,
        grid=(M // tm,),
        in_specs=[
            pl.BlockSpec((tm, D), lambda i: (i, 0)),
            pl.BlockSpec((sub, D), lambda i: (jnp.maximum(i * (tm // sub) - 1, 0), 0)),
            pl.BlockSpec((1, D), lambda i: (0, 0)),
            pl.BlockSpec((1, 1, D), lambda i: (i // per, 0, 0)),
            pl.BlockSpec((1, 1, D), lambda i: (i // per, 0, 0)),
            pl.BlockSpec((n_mix, D), lambda i: (0, 0)),
        ],
        out_specs=[pl.BlockSpec((tm, D), lambda i: (i, 0))] * n_mix,
        compiler_params=_cparams(("arbitrary",)),
        name="premix",
    )(x, x, g.reshape(1, D), sh, sc, mu)


def _mm_kernel(epilogue, nk, a_ref, w_ref, *rest):
    if epilogue == "resid":
        x_ref, gt_ref, o_ref, acc_ref = rest
    else:
        o_ref, acc_ref = rest
    k = pl.program_id(2)
    part = jnp.dot(a_ref[...], w_ref[...], preferred_element_type=F32)

    def finish(acc):
        if epilogue == "resid":
            o_ref[...] = x_ref[...] + gt_ref[0] * acc
        elif epilogue == "relu2":
            o_ref[...] = jnp.square(jnp.maximum(acc, 0.0)).astype(o_ref.dtype)
        else:
            o_ref[...] = acc.astype(o_ref.dtype)

    if nk == 1:
        finish(part)
    else:
        @pl.when(k == 0)
        def _():
            acc_ref[...] = part

        @pl.when(jnp.logical_and(k > 0, k < nk - 1))
        def _():
            acc_ref[...] += part

        @pl.when(k == nk - 1)
        def _():
            finish(acc_ref[...] + part)


def matmul(a, w, *, epilogue="none", out_dtype=F32, resid=None, gate=None,
           rows_per_batch=None, tm=1024, tn=1024, tk=2048):
    M, K = a.shape
    N = w.shape[1]
    tk = min(tk, K)
    tn = min(tn, N)
    nk = K // tk
    in_specs = [
        pl.BlockSpec((tm, tk), lambda i, j, k: (i, k)),
        pl.BlockSpec((tk, tn), lambda i, j, k: (k, j)),
    ]
    args = [a, w]
    if epilogue == "resid":
        per = rows_per_batch // tm
        in_specs += [
            pl.BlockSpec((tm, tn), lambda i, j, k: (i, j)),
            pl.BlockSpec((1, 1, tn), lambda i, j, k: (i // per, 0, j)),
        ]
        args += [resid, gate]
    acc_shape = (tm, tn) if nk > 1 else (8, LANES)
    return pl.pallas_call(
        functools.partial(_mm_kernel, epilogue, nk),
        out_shape=jax.ShapeDtypeStruct((M, N), out_dtype),
        grid=(M // tm, N // tn, nk),
        in_specs=in_specs,
        out_specs=pl.BlockSpec((tm, tn), lambda i, j, k: (i, j)),
        scratch_shapes=[pltpu.VMEM(acc_shape, F32)],
        compiler_params=_cparams(("arbitrary", "arbitrary", "arbitrary")),
        name="mm_" + epilogue,
    )(*args)


def _lora_kernel(act, a_ref, w1_ref, w2_ref, b_ref, o_ref):
    t = jnp.dot(a_ref[...], w1_ref[...], preferred_element_type=F32)
    if act == "tanh":
        t = jnp.tanh(t)
    elif act == "sigmoid":
        t = jax.nn.sigmoid(t)
    o_ref[...] = jnp.dot(t.astype(BF16), w2_ref[...], preferred_element_type=F32) + b_ref[...]


def lora(a, w1, w2, bias, act, tm=512):
    M, D = a.shape
    R = w1.shape[1]
    N = w2.shape[1]
    return pl.pallas_call(
        functools.partial(_lora_kernel, act),
        out_shape=jax.ShapeDtypeStruct((M, N), F32),
        grid=(M // tm,),
        in_specs=[
            pl.BlockSpec((tm, D), lambda i: (i, 0)),
            pl.BlockSpec((D, R), lambda i: (0, 0)),
            pl.BlockSpec((R, N), lambda i: (0, 0)),
            pl.BlockSpec((1, N), lambda i: (0, 0)),
        ],
        out_specs=pl.BlockSpec((tm, N), lambda i: (i, 0)),
        compiler_params=_cparams(("arbitrary",)),
        name="lora_" + act,
    )(a, w1, w2, bias.reshape(1, N))


def _pad_rank(w1, w2):
    r = w1.shape[1]
    rp = -(-r // LANES) * LANES
    if rp != r:
        w1 = jnp.pad(w1, ((0, 0), (0, rp - r)))
        w2 = jnp.pad(w2, ((0, rp - r), (0, 0)))
    return w1.astype(BF16), w2.astype(BF16)


def _mulcast_kernel(y_ref, g_ref, o_ref):
    o_ref[...] = (y_ref[...] * g_ref[...]).astype(o_ref.dtype)


def mulcast(y, g, tm=512):
    M, D = y.shape
    spec = pl.BlockSpec((tm, D), lambda i: (i, 0))
    return pl.pallas_call(
        _mulcast_kernel,
        out_shape=jax.ShapeDtypeStruct((M, D), BF16),
        grid=(M // tm,),
        in_specs=[spec, spec],
        out_specs=spec,
        compiler_params=_cparams(("arbitrary",)),
        name="mulcast",
    )(y, g)


def _transpose_lane_groups(xs):
    nl = xs[0].shape[-1]
    n = len(xs)
    width = nl // n
    grp = lax.broadcasted_iota(jnp.int32, xs[0].shape, 1) // width
    ys = []
    for c in range(n):
        acc = None
        for b in range(n):
            shift = ((b - c) * width) % nl
            piece = pltpu.roll(xs[b], shift, axis=1) if shift else xs[b]
            acc = piece if acc is None else jnp.where(grp == b, piece, acc)
        ys.append(acc)
    return ys


def _rows_to_head_lanes(x_ref, dst_ref):
    nb, tb, d = x_ref.shape
    ng = d // LANES
    for g in range(ng):
        xs = [x_ref[b, :, g * LANES:(g + 1) * LANES] for b in range(nb)]
        for c, y in enumerate(_transpose_lane_groups(xs)):
            dst_ref[pl.ds((c * ng + g) * tb, tb), :] = y


def _head_lanes_to_rows(src_ref, o_ref):
    nb, tb, d = o_ref.shape
    ng = d // LANES
    for g in range(ng):
        ys = [src_ref[pl.ds((c * ng + g) * tb, tb), :] for c in range(nb)]
        for b, x in enumerate(_transpose_lane_groups(ys)):
            o_ref[b, :, g * LANES:(g + 1) * LANES] = x


def _scan_kernel(r_ref, wl_ref, k_ref, v_ref, al_ref, kk_ref, ka_ref, rk_ref,
                 lnw_ref, lnb_ref, o_ref,
                 s_ref, r_s, v_s, dec_s, kap_s, nka_s, km_s, y_s, yt_s):
    tb = r_ref.shape[1]
    nl = r_s.shape[1]
    dh = r_s.shape[0] // tb
    half = dh // 2

    @pl.when(pl.program_id(0) == 0)
    def _():
        s_ref[...] = jnp.zeros_like(s_ref)

    _rows_to_head_lanes(r_ref, r_s)
    _rows_to_head_lanes(v_ref, v_s)
    _rows_to_head_lanes(wl_ref, dec_s)
    _rows_to_head_lanes(al_ref, nka_s)
    _rows_to_head_lanes(k_ref, km_s)

    def seq(ref):
        return ref[...].reshape(dh, tb, nl)

    def put(ref, val):
        ref[...] = val.reshape(dh * tb, nl)

    z = -seq(dec_s)
    softplus = jnp.maximum(z, 0.0) + jnp.log1p(jnp.exp(-jnp.abs(z)))
    put(dec_s, jnp.exp(-jnp.exp(-softplus - 0.5)))
    a = jax.nn.sigmoid(seq(nka_s))
    k = seq(km_s)
    kk = k * kk_ref[...]
    nrm = jnp.sqrt(jnp.sum(kk * kk, axis=0, keepdims=True))
    kap = kk / jnp.maximum(nrm, 1e-12)
    put(kap_s, kap)
    put(nka_s, -(kap * a))
    put(km_s, k * (1 + (a - 1) * ka_ref[...]))

    def row(ref, t, c):
        return jnp.broadcast_to(ref[pl.ds(c * tb + t, 1), :], (half, nl))

    def tile(hf, t):
        return pl.ds(hf * half * tb + t, half, stride=tb)

    zeros = jnp.zeros((half, nl), F32)

    def first_sa(hf):
        def body(g, acc):
            for j in range(SCAN_KU):
                c = g * SCAN_KU + j
                acc = acc + s_ref[hf, c] * row(kap_s, 0, c)
            return acc
        return lax.fori_loop(0, dh // SCAN_KU, body, zeros)

    def run_half(hf, t, t_next, sa):
        v_t = v_s[tile(hf, t), :]

        def body(g, carry):
            y, sa_next = carry
            for j in range(SCAN_KU):
                c = g * SCAN_KU + j
                s_new = (s_ref[hf, c] * row(dec_s, t, c) + sa * row(nka_s, t, c)
                         + v_t * row(km_s, t, c))
                s_ref[hf, c] = s_new
                y = y + s_new * row(r_s, t, c)
                sa_next = sa_next + s_new * row(kap_s, t_next, c)
            return y, sa_next

        y, sa_next = lax.fori_loop(0, dh // SCAN_KU, body, (zeros, zeros))
        yt_s[pl.ds(pl.multiple_of(t * dh + hf * half, half), half), :] = y
        return sa_next

    def step(t, sa):
        t_next = jnp.minimum(t + 1, tb - 1)
        return run_half(0, t, t_next, sa[0]), run_half(1, t, t_next, sa[1])

    lax.fori_loop(0, tb, step, (first_sa(0), first_sa(1)))

    for ch in range(dh):
        y_s[pl.ds(ch * tb, tb), :] = yt_s[pl.ds(ch, tb, stride=dh), :]
    y = seq(y_s)
    mean = jnp.mean(y, axis=0, keepdims=True)
    d = y - mean
    var = jnp.mean(d * d, axis=0, keepdims=True)
    yn = d * lax.rsqrt(var + GN_EPS) * lnw_ref[...] + lnb_ref[...]
    bonus = jnp.sum(seq(r_s) * seq(km_s) * rk_ref[...], axis=0, keepdims=True)
    put(y_s, yn + bonus * seq(v_s))
    _head_lanes_to_rows(y_s, o_ref)


def rwkv_scan(r, wl, k, v, al, k_k, k_a, r_k, ln_w, ln_b):
    B, T, D = r.shape
    dh, _, nl = k_k.shape
    tb = SCAN_TB
    seq = pl.BlockSpec((B, tb, D), lambda i: (0, i, 0))
    par = pl.BlockSpec((dh, 1, nl), lambda i: (0, 0, 0))
    return pl.pallas_call(
        _scan_kernel,
        out_shape=jax.ShapeDtypeStruct((B, T, D), F32),
        grid=(T // tb,),
        in_specs=[seq] * 5 + [par] * 5,
        out_specs=seq,
        scratch_shapes=([pltpu.VMEM((2, dh, dh // 2, nl), F32)]
                        + [pltpu.VMEM((dh * tb, nl), F32)] * 8),
        compiler_params=_cparams(("arbitrary",)),
        name="rwkv_scan",
    )(r, wl, k, v, al, k_k, k_a, r_k, ln_w, ln_b)


def _bias_kernel(g_ref, o_ref):
    nq, nw = o_ref.shape[1], o_ref.shape[2]
    ng = g_ref.shape[2]
    row = lax.broadcasted_iota(jnp.int32, (nq, ng), 0)
    x = jnp.broadcast_to(g_ref[0], (nq, ng))
    shift = 1
    while shift < nq:
        x = jnp.where((row & shift) != 0, pltpu.roll(x, shift, axis=1), x)
        shift *= 2
    qi = lax.broadcasted_iota(jnp.int32, (nq, nw), 0) // CHUNK
    kj = lax.broadcasted_iota(jnp.int32, (nq, nw), 1) // CHUNK
    band = jnp.logical_and(kj >= qi, kj <= qi + N_LEFT_CHUNKS)
    o_ref[0] = jnp.where(band, x[:, :nw], MASK_VALUE)


def band_bias(rel_bias):
    H = rel_bias.shape[0]
    ng = -(-(ATTN_W + ATTN_Q) // LANES) * LANES
    far = rel_bias[:, 2 * REL_CLIP:]
    near = rel_bias[:, 2 * REL_CLIP - 1:REL_CLIP + ATTN_PAD - ATTN_W:-1]
    n_far = ATTN_W - near.shape[1]
    g = jnp.concatenate(
        [jnp.broadcast_to(far, (H, n_far)), near, jnp.broadcast_to(far, (H, ng - ATTN_W))], axis=1)
    return pl.pallas_call(
        _bias_kernel,
        out_shape=jax.ShapeDtypeStruct((H, ATTN_Q, ATTN_W), F32),
        grid=(H,),
        in_specs=[pl.BlockSpec((1, 1, ng), lambda h: (h, 0, 0))],
        out_specs=pl.BlockSpec((1, ATTN_Q, ATTN_W), lambda h: (h, 0, 0)),
        compiler_params=_cparams(("arbitrary",)),
        name="band_bias",
    )(g.reshape(H, 1, ng))


def _attn_kernel(q_ref, k_ref, v_ref, b_ref, o_ref):
    T = q_ref.shape[1]
    n_steps = T // ATTN_Q
    n_masked = -(-ATTN_PAD // ATTN_Q)
    lane = lax.broadcasted_iota(jnp.int32, (ATTN_Q, LANES), 1)
    scale = HEAD_DIM ** -0.5
    head_scale = [jnp.where(lane < HEAD_DIM, scale, 0.0).astype(BF16),
                  jnp.where(lane >= HEAD_DIM, scale, 0.0).astype(BF16)]
    kpos = lax.broadcasted_iota(jnp.int32, (2 * ATTN_Q, ATTN_W), 1)

    def step(n, masked):
        start = pl.multiple_of(n * ATTN_Q, ATTN_Q)
        q2 = q_ref[0, pl.ds(start, ATTN_Q), :]
        kw = k_ref[0, pl.ds(start, ATTN_W), :]
        vw = v_ref[0, pl.ds(start, ATTN_W), :]
        qs = jnp.concatenate([q2 * head_scale[0], q2 * head_scale[1]], axis=0)
        s = lax.dot_general(qs, kw, (((1,), (1,)), ((), ())), preferred_element_type=F32)
        s = s + b_ref[...].reshape(2 * ATTN_Q, ATTN_W)
        if masked:
            s = jnp.where(kpos + start >= ATTN_PAD, s, MASK_VALUE)
        m = jnp.max(s, axis=-1, keepdims=True)
        p = jnp.exp(s - m)
        l = jnp.sum(p, axis=-1, keepdims=True)
        o = jnp.dot(p.astype(BF16), vw, preferred_element_type=F32) / l
        o_ref[0, pl.ds(start, ATTN_Q), :] = jnp.where(
            lane < HEAD_DIM, o[:ATTN_Q], o[ATTN_Q:]).astype(o_ref.dtype)

    def masked_body(n, carry):
        step(n, True)
        return carry

    def body(n, carry):
        step(n, False)
        return carry

    lax.fori_loop(0, n_masked, masked_body, 0)
    lax.fori_loop(n_masked, n_steps, body, 0, unroll=4)


def band_attention(q, kvp, bias):
    B, T, D = q.shape
    Tp = kvp.shape[1]
    n_pairs = D // LANES
    return pl.pallas_call(
        _attn_kernel,
        out_shape=jax.ShapeDtypeStruct((B, T, D), BF16),
        grid=(B, n_pairs),
        in_specs=[
            pl.BlockSpec((1, T, LANES), lambda b, p: (b, 0, p)),
            pl.BlockSpec((1, Tp, LANES), lambda b, p: (b, 0, p)),
            pl.BlockSpec((1, Tp, LANES), lambda b, p: (b, 0, p + n_pairs)),
            pl.BlockSpec((2, ATTN_Q, ATTN_W), lambda b, p: (p, 0, 0)),
        ],
        out_specs=pl.BlockSpec((1, T, LANES), lambda b, p: (b, 0, p)),
        compiler_params=_cparams(("arbitrary", "arbitrary")),
        name="band_attention",
    )(q, kvp, kvp, bias)


def _group_cols(w, H, group):
    lead = w.shape[:-1]
    w = w.reshape(lead + (H, HEAD_DIM // group, group))
    w = jnp.moveaxis(w, -3, -1)
    return w.reshape(lead + (H * HEAD_DIM,))


def _param_head_lanes(p, B, H, group):
    t = jnp.transpose(p.astype(F32).reshape(H, HEAD_DIM // group, group), (2, 1, 0))
    return jnp.tile(t.reshape(HEAD_DIM, H), (1, B)).reshape(HEAD_DIM, 1, B * H)


def kernel(x, c, w_ada, b_ada, g_mix, g_mlp, w_up, w_down, rwkv_mu, rwkv_w_r, rwkv_w_k, rwkv_w_v, rwkv_w_o, rwkv_w0, rwkv_w1, rwkv_w2, rwkv_a0, rwkv_a1, rwkv_a2, rwkv_g1, rwkv_g2, rwkv_k_k, rwkv_k_a, rwkv_r_k, rwkv_ln_w, rwkv_ln_b, attn_w_q, attn_w_o, attn_rel_bias, w_ada_kv, b_ada_kv, g_kv, w_k_shared, w_v_shared, g_final):
    B, T, D = x.shape
    H = D // HEAD_DIM
    M = B * T
    depth = w_ada.shape[0]
    n_a = rwkv_mu.shape[0]
    group = LANES // H
    assert T % ATTN_Q == 0 and T % SCAN_TB == 0 and B * H == LANES and group == B

    xf = x.reshape(M, D)
    c_pad = jnp.pad(c, ((0, 16 - B), (0, 0))).astype(BF16)
    mods = ada_mod(c_pad, w_ada, b_ada)[:, :B]
    mod_kv = ada_mod(c_pad, w_ada_kv[None], b_ada_kv[None])[0, :B]

    def split(m, n):
        return [p.reshape(B, 1, D) for p in jnp.split(m, n, axis=-1)]

    kvp = None
    for layer in range(depth):
        sh1, sc1, gt1, sh2, sc2, gt2 = split(mods[layer], 6)
        if layer == n_a:
            sh_kv, sc_kv = split(mod_kv, 2)
            h_kv = norm_mod(xf, g_kv, sh_kv, sc_kv, T)
            w_kv = jnp.concatenate([w_k_shared, w_v_shared], axis=1).astype(BF16)
            kv = matmul(h_kv, w_kv, out_dtype=BF16).reshape(B, T, 2 * D)
            kvp = jnp.pad(kv, ((0, 0), (ATTN_PAD, 0), (0, 0)))
        if layer < n_a:
            i = layer
            gc = functools.partial(_group_cols, H=H, group=group)
            hp = functools.partial(_param_head_lanes, B=B, H=H, group=group)
            xr, xw, xk, xv, xa, xg = premix(xf, g_mix[layer], sh1, sc1, rwkv_mu[i], T)
            r = matmul(xr, gc(rwkv_w_r[i]).astype(BF16), **SQUARE_TILES)
            k = matmul(xk, gc(rwkv_w_k[i]).astype(BF16), **SQUARE_TILES)
            v = matmul(xv, gc(rwkv_w_v[i]).astype(BF16), **SQUARE_TILES)
            wl = lora(xw, *_pad_rank(rwkv_w1[i], gc(rwkv_w2[i])), gc(rwkv_w0[i]), "tanh")
            al = lora(xa, *_pad_rank(rwkv_a1[i], gc(rwkv_a2[i])), gc(rwkv_a0[i]), "none")
            g = lora(xg, *_pad_rank(rwkv_g1[i], gc(rwkv_g2[i])), jnp.zeros((D,), F32), "sigmoid")
            rows = lambda a: a.reshape(B, T, D)
            y = rwkv_scan(rows(r), rows(wl), rows(k), rows(v), rows(al),
                          hp(rwkv_k_k[i]), hp(rwkv_k_a[i]), hp(rwkv_r_k[i].reshape(D)),
                          hp(rwkv_ln_w[i]), hp(rwkv_ln_b[i]))
            yg = mulcast(y.reshape(M, D), g)
            w_o = gc(rwkv_w_o[i].T).T
        else:
            i = layer - n_a
            h = norm_mod(xf, g_mix[layer], sh1, sc1, T)
            q = matmul(h, attn_w_q[i].astype(BF16), out_dtype=BF16, **SQUARE_TILES).reshape(B, T, D)
            bias = band_bias(attn_rel_bias[i].astype(F32))
            yg = band_attention(q, kvp, bias).reshape(M, D)
            w_o = attn_w_o[i]
        xf = matmul(yg, w_o.astype(BF16), epilogue="resid", resid=xf, gate=gt1,
                    rows_per_batch=T, **SQUARE_TILES)
        h = norm_mod(xf, g_mlp[layer], sh2, sc2, T)
        hid = matmul(h, w_up[layer].astype(BF16), epilogue="relu2", out_dtype=BF16)
        xf = matmul(hid, w_down[layer].astype(BF16), epilogue="resid", resid=xf, gate=gt2,
                    rows_per_batch=T)
    return final_norm(xf, g_final).reshape(B, T, D)
```

```python
import functools

import jax
import jax.numpy as jnp
from jax import lax
from jax.experimental import pallas as pl
from jax.experimental.pallas import tpu as pltpu

F32 = jnp.float32
BF16 = jnp.bfloat16

CHUNK = 64
N_LEFT_CHUNKS = 8
HEAD_DIM = 64
REL_CLIP = 256
RMS_EPS = 1e-6
GN_EPS = 64e-5
MASK_VALUE = -1e30

LANES = 128
ATTN_Q = 2 * CHUNK
ATTN_PAD = N_LEFT_CHUNKS * CHUNK
ATTN_W = ATTN_PAD + ATTN_Q
SCAN_TB = 32
SCAN_KU = 16
VMEM_LIMIT = 48 * 1024 * 1024
SQUARE_TILES = dict(tm=512, tn=2048)


def _cparams(sem):
    return pltpu.CompilerParams(dimension_semantics=sem, vmem_limit_bytes=VMEM_LIMIT)


def _ada_kernel(c_ref, w_ref, b_ref, o_ref):
    w = w_ref[0].astype(BF16)
    o_ref[0] = jnp.dot(c_ref[...], w, preferred_element_type=F32) + b_ref[0]


def ada_mod(c_pad, w, b, tn=1024):
    L, D, N = w.shape
    R = c_pad.shape[0]
    return pl.pallas_call(
        _ada_kernel,
        out_shape=jax.ShapeDtypeStruct((L, R, N), F32),
        grid=(L, N // tn),
        in_specs=[
            pl.BlockSpec((R, D), lambda l, j: (0, 0)),
            pl.BlockSpec((1, D, tn), lambda l, j: (l, 0, j)),
            pl.BlockSpec((1, 1, tn), lambda l, j: (l, 0, j)),
        ],
        out_specs=pl.BlockSpec((1, R, tn), lambda l, j: (l, 0, j)),
        compiler_params=_cparams(("arbitrary", "arbitrary")),
        name="ada_mod",
    )(c_pad, w, b.reshape(L, 1, N))


def _rms(x, g):
    ms = jnp.mean(x * x, axis=-1, keepdims=True)
    return x * lax.rsqrt(ms + RMS_EPS) * g


def _norm_mod_kernel(x_ref, g_ref, sh_ref, sc_ref, o_ref):
    y = _rms(x_ref[...], g_ref[...])
    o_ref[...] = (y * (1 + sc_ref[0]) + sh_ref[0]).astype(o_ref.dtype)


def norm_mod(x, g, sh, sc, rows_per_batch, tm=512):
    M, D = x.shape
    per = rows_per_batch // tm
    return pl.pallas_call(
        _norm_mod_kernel,
        out_shape=jax.ShapeDtypeStruct((M, D), BF16),
        grid=(M // tm,),
        in_specs=[
            pl.BlockSpec((tm, D), lambda i: (i, 0)),
            pl.BlockSpec((1, D), lambda i: (0, 0)),
            pl.BlockSpec((1, 1, D), lambda i: (i // per, 0, 0)),
            pl.BlockSpec((1, 1, D), lambda i: (i // per, 0, 0)),
        ],
        out_specs=pl.BlockSpec((tm, D), lambda i: (i, 0)),
        compiler_params=_cparams(("arbitrary",)),
        name="norm_mod",
    )(x, g.reshape(1, D), sh, sc)


def _final_norm_kernel(x_ref, g_ref, o_ref):
    o_ref[...] = _rms(x_ref[...], g_ref[...])


def final_norm(x, g, tm=512):
    M, D = x.shape
    return pl.pallas_call(
        _final_norm_kernel,
        out_shape=jax.ShapeDtypeStruct((M, D), F32),
        grid=(M // tm,),
        in_specs=[
            pl.BlockSpec((tm, D), lambda i: (i, 0)),
            pl.BlockSpec((1, D), lambda i: (0, 0)),
        ],
        out_specs=pl.BlockSpec((tm, D), lambda i: (i, 0)),
        compiler_params=_cparams(("arbitrary",)),
        name="final_norm",
    )(x, g.reshape(1, D))


def _premix_kernel(per, x_ref, xp_ref, g_ref, sh_ref, sc_ref, mu_ref, *o_refs):
    g = g_ref[...]
    sc = 1 + sc_ref[0]
    sh = sh_ref[0]
    h = _rms(x_ref[...], g) * sc + sh
    hp = (_rms(xp_ref[...], g) * sc + sh)[7:8, :]
    first_tile = (pl.program_id(0) % per) == 0
    hp = jnp.where(first_tile, 0.0, hp)
    row = lax.broadcasted_iota(jnp.int32, h.shape, 0)
    hs = jnp.where(row == 0, hp, pltpu.roll(h, 1, axis=0))
    xx = hs - h
    for j, o_ref in enumerate(o_refs):
        o_ref[...] = (h + xx * mu_ref[j:j + 1, :]).astype(o_ref.dtype)


def premix(x, g, sh, sc, mu, rows_per_batch, tm=256):
    M, D = x.shape
    per = rows_per_batch // tm
    sub = 8
    n_mix = mu.shape[0]
    return pl.pallas_call(
        functools.partial(_premix_kernel, per),
        out_shape=[jax.ShapeDtypeStruct((M, D), BF16)] * n_mix,
        grid=(M // tm,),
        in_specs=[
            pl.BlockSpec((tm, D), lambda i: (i, 0)),
            pl.BlockSpec((sub, D), lambda i: (jnp.maximum(i * (tm // sub) - 1, 0), 0)),
            pl.BlockSpec((1, D), lambda i: (0, 0)),
            pl.BlockSpec((1, 1, D), lambda i: (i // per, 0, 0)),
            pl.BlockSpec((1, 1, D), lambda i: (i // per, 0, 0)),
            pl.BlockSpec((n_mix, D), lambda i: (0, 0)),
        ],
        out_specs=[pl.BlockSpec((tm, D), lambda i: (i, 0))] * n_mix,
        compiler_params=_cparams(("arbitrary",)),
        name="premix",
    )(x, x, g.reshape(1, D), sh, sc, mu)


def _mm_kernel(epilogue, nk, a_ref, w_ref, *rest):
    if epilogue == "resid":
        x_ref, gt_ref, o_ref, acc_ref = rest
    else:
        o_ref, acc_ref = rest
    k = pl.program_id(2)
    part = jnp.dot(a_ref[...], w_ref[...], preferred_element_type=F32)

    def finish(acc):
        if epilogue == "resid":
            o_ref[...] = x_ref[...] + gt_ref[0] * acc
        elif epilogue == "relu2":
            o_ref[...] = jnp.square(jnp.maximum(acc, 0.0)).astype(o_ref.dtype)
        else:
            o_ref[...] = acc.astype(o_ref.dtype)

    if nk == 1:
        finish(part)
    else:
        @pl.when(k == 0)
        def _():
            acc_ref[...] = part

        @pl.when(jnp.logical_and(k > 0, k < nk - 1))
        def _():
            acc_ref[...] += part

        @pl.when(k == nk - 1)
        def _():
            finish(acc_ref[...] + part)


def matmul(a, w, *, epilogue="none", out_dtype=F32, resid=None, gate=None,
           rows_per_batch=None, tm=1024, tn=1024, tk=2048):
    M, K = a.shape
    N = w.shape[1]
    tk = min(tk, K)
    tn = min(tn, N)
    nk = K // tk
    in_specs = [
        pl.BlockSpec((tm, tk), lambda i, j, k: (i, k)),
        pl.BlockSpec((tk, tn), lambda i, j, k: (k, j)),
    ]
    args = [a, w]
    if epilogue == "resid":
        per = rows_per_batch // tm
        in_specs += [
            pl.BlockSpec((tm, tn), lambda i, j, k: (i, j)),
            pl.BlockSpec((1, 1, tn), lambda i, j, k: (i // per, 0, j)),
        ]
        args += [resid, gate]
    acc_shape = (tm, tn) if nk > 1 else (8, LANES)
    return pl.pallas_call(
        functools.partial(_mm_kernel, epilogue, nk),
        out_shape=jax.ShapeDtypeStruct((M, N), out_dtype),
        grid=(M // tm, N // tn, nk),
        in_specs=in_specs,
        out_specs=pl.BlockSpec((tm, tn), lambda i, j, k: (i, j)),
        scratch_shapes=[pltpu.VMEM(acc_shape, F32)],
        compiler_params=_cparams(("arbitrary", "arbitrary", "arbitrary")),
        name="mm_" + epilogue,
    )(*args)


def _lora_kernel(act, a_ref, w1_ref, w2_ref, b_ref, o_ref):
    t = jnp.dot(a_ref[...], w1_ref[...], preferred_element_type=F32)
    if act == "tanh":
        t = jnp.tanh(t)
    elif act == "sigmoid":
        t = jax.nn.sigmoid(t)
    o_ref[...] = jnp.dot(t.astype(BF16), w2_ref[...], preferred_element_type=F32) + b_ref[...]


def lora(a, w1, w2, bias, act, tm=512):
    M, D = a.shape
    R = w1.shape[1]
    N = w2.shape[1]
    return pl.pallas_call(
        functools.partial(_lora_kernel, act),
        out_shape=jax.ShapeDtypeStruct((M, N), F32),
        grid=(M // tm,),
        in_specs=[
            pl.BlockSpec((tm, D), lambda i: (i, 0)),
            pl.BlockSpec((D, R), lambda i: (0, 0)),
            pl.BlockSpec((R, N), lambda i: (0, 0)),
            pl.BlockSpec((1, N), lambda i: (0, 0)),
        ],
        out_specs=pl.BlockSpec((tm, N), lambda i: (i, 0)),
        compiler_params=_cparams(("arbitrary",)),
        name="lora_" + act,
    )(a, w1, w2, bias.reshape(1, N))


def _pad_rank(w1, w2):
    r = w1.shape[1]
    rp = -(-r // LANES) * LANES
    if rp != r:
        w1 = jnp.pad(w1, ((0, 0), (0, rp - r)))
        w2 = jnp.pad(w2, ((0, rp - r), (0, 0)))
    return w1.astype(BF16), w2.astype(BF16)


def _transpose_lane_groups(xs):
    nl = xs[0].shape[-1]
    n = len(xs)
    width = nl // n
    grp = lax.broadcasted_iota(jnp.int32, xs[0].shape, 1) // width
    ys = []
    for c in range(n):
        acc = None
        for b in range(n):
            shift = ((b - c) * width) % nl
            piece = pltpu.roll(xs[b], shift, axis=1) if shift else xs[b]
            acc = piece if acc is None else jnp.where(grp == b, piece, acc)
        ys.append(acc)
    return ys


def _rows_to_head_lanes(x, nb, o_ref):
    tq = x.shape[0] // nb
    ng = x.shape[1] // LANES
    for g in range(ng):
        xs = [x[b * tq:(b + 1) * tq, g * LANES:(g + 1) * LANES] for b in range(nb)]
        for c, y in enumerate(_transpose_lane_groups(xs)):
            o_ref[c * ng + g] = y


def _head_lanes_to_rows(y_ref, nb):
    ng = y_ref.shape[0] // nb
    cols = [_transpose_lane_groups([y_ref[c * ng + g] for c in range(nb)]) for g in range(ng)]
    return [jnp.concatenate([cols[g][b] for g in range(ng)], axis=1) for b in range(nb)]


def _proj_hl_kernel(a_ref, w_ref, o_ref):
    nb, tq, kdim = a_ref.shape
    acc = jnp.dot(a_ref[...].reshape(nb * tq, kdim), w_ref[...], preferred_element_type=F32)
    _rows_to_head_lanes(acc, nb, o_ref)


def proj_head_lanes(a, w, tq=128):
    B, T, K = a.shape
    D = w.shape[1]
    return pl.pallas_call(
        _proj_hl_kernel,
        out_shape=jax.ShapeDtypeStruct((HEAD_DIM, T, LANES), F32),
        grid=(T // tq,),
        in_specs=[
            pl.BlockSpec((B, tq, K), lambda i: (0, i, 0)),
            pl.BlockSpec((K, D), lambda i: (0, 0)),
        ],
        out_specs=pl.BlockSpec((HEAD_DIM, tq, LANES), lambda i: (0, i, 0)),
        compiler_params=_cparams(("arbitrary",)),
        name="proj_head_lanes",
    )(a, w)


def _lora_hl_kernel(act, a_ref, w1_ref, w2_ref, b_ref, o_ref):
    nb, tq, kdim = a_ref.shape
    t = jnp.dot(a_ref[...].reshape(nb * tq, kdim), w1_ref[...], preferred_element_type=F32)
    if act == "tanh":
        t = jnp.tanh(t)
    o = jnp.dot(t.astype(BF16), w2_ref[...], preferred_element_type=F32) + b_ref[...]
    _rows_to_head_lanes(o, nb, o_ref)


def lora_head_lanes(a, w1, w2, bias, act, tq=128):
    B, T, D = a.shape
    R = w1.shape[1]
    N = w2.shape[1]
    return pl.pallas_call(
        functools.partial(_lora_hl_kernel, act),
        out_shape=jax.ShapeDtypeStruct((HEAD_DIM, T, LANES), F32),
        grid=(T // tq,),
        in_specs=[
            pl.BlockSpec((B, tq, D), lambda i: (0, i, 0)),
            pl.BlockSpec((D, R), lambda i: (0, 0)),
            pl.BlockSpec((R, N), lambda i: (0, 0)),
            pl.BlockSpec((1, N), lambda i: (0, 0)),
        ],
        out_specs=pl.BlockSpec((HEAD_DIM, tq, LANES), lambda i: (0, i, 0)),
        compiler_params=_cparams(("arbitrary",)),
        name="lora_hl_" + act,
    )(a, w1, w2, bias.reshape(1, N))


def _out_proj_hl_kernel(y_ref, g_ref, w_ref, x_ref, gt_ref, o_ref):
    nb, tq, d = g_ref.shape
    ys = _head_lanes_to_rows(y_ref, nb)
    a = jnp.concatenate([(ys[b] * g_ref[b]).astype(BF16) for b in range(nb)], axis=0)
    acc = jnp.dot(a, w_ref[...], preferred_element_type=F32)
    o_ref[...] = x_ref[...] + gt_ref[...] * acc.reshape(nb, tq, acc.shape[1])


def out_proj_head_lanes(y, g, w, x, gate, tq=64):
    B, T, D = g.shape
    N = w.shape[1]
    return pl.pallas_call(
        _out_proj_hl_kernel,
        out_shape=jax.ShapeDtypeStruct((B, T, N), F32),
        grid=(T // tq,),
        in_specs=[
            pl.BlockSpec((HEAD_DIM, tq, LANES), lambda i: (0, i, 0)),
            pl.BlockSpec((B, tq, D), lambda i: (0, i, 0)),
            pl.BlockSpec((D, N), lambda i: (0, 0)),
            pl.BlockSpec((B, tq, N), lambda i: (0, i, 0)),
            pl.BlockSpec((B, 1, N), lambda i: (0, 0, 0)),
        ],
        out_specs=pl.BlockSpec((B, tq, N), lambda i: (0, i, 0)),
        compiler_params=_cparams(("arbitrary",)),
        name="out_proj_head_lanes",
    )(y, g, w, x, gate)


def _scan_kernel(r_ref, wl_ref, k_ref, v_ref, al_ref, kk_ref, ka_ref, rk_ref,
                 lnw_ref, lnb_ref, o_ref,
                 s_ref, v_s, dec_s, kap_s, nka_s, km_s, yt_s):
    dh, tb, nl = r_ref.shape
    half = dh // 2

    @pl.when(pl.program_id(0) == 0)
    def _():
        s_ref[...] = jnp.zeros_like(s_ref)

    def seq(ref):
        return ref[...].reshape(dh, tb, nl)

    def put(ref, val):
        ref[...] = val.reshape(dh * tb, nl)

    put(v_s, v_ref[...])
    z = -wl_ref[...]
    softplus = jnp.maximum(z, 0.0) + jnp.log1p(jnp.exp(-jnp.abs(z)))
    put(dec_s, jnp.exp(-jnp.exp(-softplus - 0.5)))
    a = jax.nn.sigmoid(al_ref[...])
    k = k_ref[...]
    kk = k * kk_ref[...]
    nrm = jnp.sqrt(jnp.sum(kk * kk, axis=0, keepdims=True))
    kap = kk / jnp.maximum(nrm, 1e-12)
    put(kap_s, kap)
    put(nka_s, -(kap * a))
    put(km_s, k * (1 + (a - 1) * ka_ref[...]))

    def row(ref, t, c):
        return jnp.broadcast_to(ref[pl.ds(c * tb + t, 1), :], (half, nl))

    def r_row(t, c):
        return jnp.broadcast_to(r_ref[c, pl.ds(t, 1), :], (half, nl))

    def tile(hf, t):
        return pl.ds(hf * half * tb + t, half, stride=tb)

    zeros = jnp.zeros((half, nl), F32)

    def first_sa(hf):
        def body(g, acc):
            for j in range(SCAN_KU):
                c = g * SCAN_KU + j
                acc = acc + s_ref[hf, c] * row(kap_s, 0, c)
            return acc
        return lax.fori_loop(0, dh // SCAN_KU, body, zeros)

    def run_half(hf, t, t_next, sa):
        v_t = v_s[tile(hf, t), :]

        def body(g, carry):
            y, sa_next = carry
            for j in range(SCAN_KU):
                c = g * SCAN_KU + j
                s_new = (s_ref[hf, c] * row(dec_s, t, c) + sa * row(nka_s, t, c)
                         + v_t * row(km_s, t, c))
                s_ref[hf, c] = s_new
                y = y + s_new * r_row(t, c)
                sa_next = sa_next + s_new * row(kap_s, t_next, c)
            return y, sa_next

        y, sa_next = lax.fori_loop(0, dh // SCAN_KU, body, (zeros, zeros))
        yt_s[pl.ds(pl.multiple_of(t * dh + hf * half, half), half), :] = y
        return sa_next

    def step(t, sa):
        t_next = jnp.minimum(t + 1, tb - 1)
        return run_half(0, t, t_next, sa[0]), run_half(1, t, t_next, sa[1])

    lax.fori_loop(0, tb, step, (first_sa(0), first_sa(1)))

    for ch in range(dh):
        o_ref[ch] = yt_s[pl.ds(ch, tb, stride=dh), :]
    y = o_ref[...]
    mean = jnp.mean(y, axis=0, keepdims=True)
    d = y - mean
    var = jnp.mean(d * d, axis=0, keepdims=True)
    yn = d * lax.rsqrt(var + GN_EPS) * lnw_ref[...] + lnb_ref[...]
    bonus = jnp.sum(r_ref[...] * seq(km_s) * rk_ref[...], axis=0, keepdims=True)
    o_ref[...] = yn + bonus * v_ref[...]


def rwkv_scan(r, wl, k, v, al, k_k, k_a, r_k, ln_w, ln_b):
    dh, T, nl = r.shape
    tb = SCAN_TB
    seq = pl.BlockSpec((dh, tb, nl), lambda i: (0, i, 0))
    par = pl.BlockSpec((dh, 1, nl), lambda i: (0, 0, 0))
    return pl.pallas_call(
        _scan_kernel,
        out_shape=jax.ShapeDtypeStruct((dh, T, nl), F32),
        grid=(T // tb,),
        in_specs=[seq] * 5 + [par] * 5,
        out_specs=seq,
        scratch_shapes=([pltpu.VMEM((2, dh, dh // 2, nl), F32)]
                        + [pltpu.VMEM((dh * tb, nl), F32)] * 6),
        compiler_params=_cparams(("arbitrary",)),
        name="rwkv_scan",
    )(r, wl, k, v, al, k_k, k_a, r_k, ln_w, ln_b)


def _bias_kernel(g_ref, o_ref):
    nq, nw = o_ref.shape[1], o_ref.shape[2]
    ng = g_ref.shape[2]
    row = lax.broadcasted_iota(jnp.int32, (nq, ng), 0)
    x = jnp.broadcast_to(g_ref[0], (nq, ng))
    shift = 1
    while shift < nq:
        x = jnp.where((row & shift) != 0, pltpu.roll(x, shift, axis=1), x)
        shift *= 2
    qi = lax.broadcasted_iota(jnp.int32, (nq, nw), 0) // CHUNK
    kj = lax.broadcasted_iota(jnp.int32, (nq, nw), 1) // CHUNK
    band = jnp.logical_and(kj >= qi, kj <= qi + N_LEFT_CHUNKS)
    o_ref[0] = jnp.where(band, x[:, :nw], MASK_VALUE)


def band_bias(rel_bias):
    H = rel_bias.shape[0]
    ng = -(-(ATTN_W + ATTN_Q) // LANES) * LANES
    far = rel_bias[:, 2 * REL_CLIP:]
    near = rel_bias[:, 2 * REL_CLIP - 1:REL_CLIP + ATTN_PAD - ATTN_W:-1]
    n_far = ATTN_W - near.shape[1]
    g = jnp.concatenate(
        [jnp.broadcast_to(far, (H, n_far)), near, jnp.broadcast_to(far, (H, ng - ATTN_W))], axis=1)
    return pl.pallas_call(
        _bias_kernel,
        out_shape=jax.ShapeDtypeStruct((H, ATTN_Q, ATTN_W), F32),
        grid=(H,),
        in_specs=[pl.BlockSpec((1, 1, ng), lambda h: (h, 0, 0))],
        out_specs=pl.BlockSpec((1, ATTN_Q, ATTN_W), lambda h: (h, 0, 0)),
        compiler_params=_cparams(("arbitrary",)),
        name="band_bias",
    )(g.reshape(H, 1, ng))


def _attn_kernel(q_ref, k_ref, v_ref, b_ref, o_ref):
    T = q_ref.shape[1]
    n_steps = T // ATTN_Q
    n_masked = -(-ATTN_PAD // ATTN_Q)
    lane = lax.broadcasted_iota(jnp.int32, (ATTN_Q, LANES), 1)
    scale = HEAD_DIM ** -0.5
    head_scale = [jnp.where(lane < HEAD_DIM, scale, 0.0).astype(BF16),
                  jnp.where(lane >= HEAD_DIM, scale, 0.0).astype(BF16)]
    kpos = lax.broadcasted_iota(jnp.int32, (2 * ATTN_Q, ATTN_W), 1)

    def step(n, masked):
        start = pl.multiple_of(n * ATTN_Q, ATTN_Q)
        q2 = q_ref[0, pl.ds(start, ATTN_Q), :]
        kw = k_ref[0, pl.ds(start, ATTN_W), :]
        vw = v_ref[0, pl.ds(start, ATTN_W), :]
        qs = jnp.concatenate([q2 * head_scale[0], q2 * head_scale[1]], axis=0)
        s = lax.dot_general(qs, kw, (((1,), (1,)), ((), ())), preferred_element_type=F32)
        s = s + b_ref[...].reshape(2 * ATTN_Q, ATTN_W)
        if masked:
            s = jnp.where(kpos + start >= ATTN_PAD, s, MASK_VALUE)
        m = jnp.max(s, axis=-1, keepdims=True)
        p = jnp.exp(s - m)
        l = jnp.sum(p, axis=-1, keepdims=True)
        o = jnp.dot(p.astype(BF16), vw, preferred_element_type=F32) / l
        o_ref[0, pl.ds(start, ATTN_Q), :] = jnp.where(
            lane < HEAD_DIM, o[:ATTN_Q], o[ATTN_Q:]).astype(o_ref.dtype)

    def masked_body(n, carry):
        step(n, True)
        return carry

    def body(n, carry):
        step(n, False)
        return carry

    lax.fori_loop(0, n_masked, masked_body, 0, unroll=2)
    lax.fori_loop(n_masked, n_steps, body, 0, unroll=4)


def band_attention(q, kvp, bias):
    B, T, D = q.shape
    Tp = kvp.shape[1]
    n_pairs = D // LANES
    return pl.pallas_call(
        _attn_kernel,
        out_shape=jax.ShapeDtypeStruct((B, T, D), BF16),
        grid=(B, n_pairs),
        in_specs=[
            pl.BlockSpec((1, T, LANES), lambda b, p: (b, 0, p)),
            pl.BlockSpec((1, Tp, LANES), lambda b, p: (b, 0, p)),
            pl.BlockSpec((1, Tp, LANES), lambda b, p: (b, 0, p + n_pairs)),
            pl.BlockSpec((2, ATTN_Q, ATTN_W), lambda b, p: (p, 0, 0)),
        ],
        out_specs=pl.BlockSpec((1, T, LANES), lambda b, p: (b, 0, p)),
        compiler_params=_cparams(("arbitrary", "arbitrary")),
        name="band_attention",
    )(q, kvp, kvp, bias)


def _group_cols(w, H, group):
    lead = w.shape[:-1]
    w = w.reshape(lead + (H, HEAD_DIM // group, group))
    w = jnp.moveaxis(w, -3, -1)
    return w.reshape(lead + (H * HEAD_DIM,))


def _param_head_lanes(p, B, H, group):
    t = jnp.transpose(p.astype(F32).reshape(H, HEAD_DIM // group, group), (2, 1, 0))
    return jnp.tile(t.reshape(HEAD_DIM, H), (1, B)).reshape(HEAD_DIM, 1, B * H)


def kernel(x, c, w_ada, b_ada, g_mix, g_mlp, w_up, w_down, rwkv_mu, rwkv_w_r, rwkv_w_k, rwkv_w_v, rwkv_w_o, rwkv_w0, rwkv_w1, rwkv_w2, rwkv_a0, rwkv_a1, rwkv_a2, rwkv_g1, rwkv_g2, rwkv_k_k, rwkv_k_a, rwkv_r_k, rwkv_ln_w, rwkv_ln_b, attn_w_q, attn_w_o, attn_rel_bias, w_ada_kv, b_ada_kv, g_kv, w_k_shared, w_v_shared, g_final):
    B, T, D = x.shape
    H = D // HEAD_DIM
    M = B * T
    depth = w_ada.shape[0]
    n_a = rwkv_mu.shape[0]
    group = LANES // H
    assert T % ATTN_Q == 0 and T % SCAN_TB == 0 and B * H == LANES and group == B

    xf = x.reshape(M, D)
    c_pad = jnp.pad(c, ((0, 16 - B), (0, 0))).astype(BF16)
    mods = ada_mod(c_pad, w_ada, b_ada)[:, :B]
    mod_kv = ada_mod(c_pad, w_ada_kv[None], b_ada_kv[None])[0, :B]

    def split(m, n):
        return [p.reshape(B, 1, D) for p in jnp.split(m, n, axis=-1)]

    kvp = None
    for layer in range(depth):
        sh1, sc1, gt1, sh2, sc2, gt2 = split(mods[layer], 6)
        if layer == n_a:
            sh_kv, sc_kv = split(mod_kv, 2)
            h_kv = norm_mod(xf, g_kv, sh_kv, sc_kv, T)
            w_kv = jnp.concatenate([w_k_shared, w_v_shared], axis=1).astype(BF16)
            kv = matmul(h_kv, w_kv, out_dtype=BF16).reshape(B, T, 2 * D)
            kvp = jnp.pad(kv, ((0, 0), (ATTN_PAD, 0), (0, 0)))
        if layer < n_a:
            i = layer
            gc = functools.partial(_group_cols, H=H, group=group)
            hp = functools.partial(_param_head_lanes, B=B, H=H, group=group)
            xr, xw, xk, xv, xa, xg = premix(xf, g_mix[layer], sh1, sc1, rwkv_mu[i], T)
            rows = lambda a: a.reshape(B, T, D)
            r = proj_head_lanes(rows(xr), gc(rwkv_w_r[i]).astype(BF16))
            k = proj_head_lanes(rows(xk), gc(rwkv_w_k[i]).astype(BF16))
            v = proj_head_lanes(rows(xv), gc(rwkv_w_v[i]).astype(BF16))
            wl = lora_head_lanes(rows(xw), *_pad_rank(rwkv_w1[i], gc(rwkv_w2[i])),
                                 gc(rwkv_w0[i]), "tanh")
            al = lora_head_lanes(rows(xa), *_pad_rank(rwkv_a1[i], gc(rwkv_a2[i])),
                                 gc(rwkv_a0[i]), "none")
            g = lora(xg, *_pad_rank(rwkv_g1[i], gc(rwkv_g2[i])), jnp.zeros((D,), F32), "sigmoid")
            y = rwkv_scan(r, wl, k, v, al,
                          hp(rwkv_k_k[i]), hp(rwkv_k_a[i]), hp(rwkv_r_k[i].reshape(D)),
                          hp(rwkv_ln_w[i]), hp(rwkv_ln_b[i]))
            w_o = gc(rwkv_w_o[i].T).T
            xf = out_proj_head_lanes(y, rows(g), w_o.astype(BF16), rows(xf), gt1).reshape(M, D)
        else:
            i = layer - n_a
            h = norm_mod(xf, g_mix[layer], sh1, sc1, T)
            q = matmul(h, attn_w_q[i].astype(BF16), out_dtype=BF16, **SQUARE_TILES).reshape(B, T, D)
            bias = band_bias(attn_rel_bias[i].astype(F32))
            yg = band_attention(q, kvp, bias).reshape(M, D)
            xf = matmul(yg, attn_w_o[i].astype(BF16), epilogue="resid", resid=xf, gate=gt1,
                        rows_per_batch=T, **SQUARE_TILES)
        h = norm_mod(xf, g_mlp[layer], sh2, sc2, T)
        hid = matmul(h, w_up[layer].astype(BF16), epilogue="relu2", out_dtype=BF16)
        xf = matmul(hid, w_down[layer].astype(BF16), epilogue="resid", resid=xf, gate=gt2,
                    rows_per_batch=T)
    return final_norm(xf, g_final).reshape(B, T, D)
```

```python
import functools

import jax
import jax.numpy as jnp
from jax import lax
from jax.experimental import pallas as pl
from jax.experimental.pallas import tpu as pltpu

F32 = jnp.float32
BF16 = jnp.bfloat16

CHUNK = 64
N_LEFT_CHUNKS = 8
HEAD_DIM = 64
REL_CLIP = 256
RMS_EPS = 1e-6
GN_EPS = 64e-5
MASK_VALUE = -1e30

LANES = 128
ATTN_Q = 2 * CHUNK
ATTN_PAD = N_LEFT_CHUNKS * CHUNK
ATTN_W = ATTN_PAD + ATTN_Q
SCAN_TB = 32
SCAN_KU = 64
VMEM_LIMIT = 48 * 1024 * 1024
SQUARE_TILES = dict(tm=512, tn=2048)


def _cparams(sem):
    return pltpu.CompilerParams(dimension_semantics=sem, vmem_limit_bytes=VMEM_LIMIT)


def _ada_kernel(c_ref, w_ref, b_ref, o_ref):
    w = w_ref[0].astype(BF16)
    o_ref[0] = jnp.dot(c_ref[...], w, preferred_element_type=F32) + b_ref[0]


def ada_mod(c_pad, w, b, tn=1024):
    L, D, N = w.shape
    R = c_pad.shape[0]
    return pl.pallas_call(
        _ada_kernel,
        out_shape=jax.ShapeDtypeStruct((L, R, N), F32),
        grid=(L, N // tn),
        in_specs=[
            pl.BlockSpec((R, D), lambda l, j: (0, 0)),
            pl.BlockSpec((1, D, tn), lambda l, j: (l, 0, j)),
            pl.BlockSpec((1, 1, tn), lambda l, j: (l, 0, j)),
        ],
        out_specs=pl.BlockSpec((1, R, tn), lambda l, j: (l, 0, j)),
        compiler_params=_cparams(("arbitrary", "arbitrary")),
        name="ada_mod",
    )(c_pad, w, b.reshape(L, 1, N))


def _rms(x, g):
    ms = jnp.mean(x * x, axis=-1, keepdims=True)
    return x * lax.rsqrt(ms + RMS_EPS) * g


def _norm_mod_kernel(x_ref, g_ref, sh_ref, sc_ref, o_ref):
    y = _rms(x_ref[...], g_ref[...])
    o_ref[...] = (y * (1 + sc_ref[0]) + sh_ref[0]).astype(o_ref.dtype)


def norm_mod(x, g, sh, sc, rows_per_batch, tm=512):
    M, D = x.shape
    per = rows_per_batch // tm
    return pl.pallas_call(
        _norm_mod_kernel,
        out_shape=jax.ShapeDtypeStruct((M, D), BF16),
        grid=(M // tm,),
        in_specs=[
            pl.BlockSpec((tm, D), lambda i: (i, 0)),
            pl.BlockSpec((1, D), lambda i: (0, 0)),
            pl.BlockSpec((1, 1, D), lambda i: (i // per, 0, 0)),
            pl.BlockSpec((1, 1, D), lambda i: (i // per, 0, 0)),
        ],
        out_specs=pl.BlockSpec((tm, D), lambda i: (i, 0)),
        compiler_params=_cparams(("arbitrary",)),
        name="norm_mod",
    )(x, g.reshape(1, D), sh, sc)


def _final_norm_kernel(x_ref, g_ref, o_ref):
    o_ref[...] = _rms(x_ref[...], g_ref[...])


def final_norm(x, g, tm=512):
    M, D = x.shape
    return pl.pallas_call(
        _final_norm_kernel,
        out_shape=jax.ShapeDtypeStruct((M, D), F32),
        grid=(M // tm,),
        in_specs=[
            pl.BlockSpec((tm, D), lambda i: (i, 0)),
            pl.BlockSpec((1, D), lambda i: (0, 0)),
        ],
        out_specs=pl.BlockSpec((tm, D), lambda i: (i, 0)),
        compiler_params=_cparams(("arbitrary",)),
        name="final_norm",
    )(x, g.reshape(1, D))


def _premix_kernel(per, x_ref, xp_ref, g_ref, sh_ref, sc_ref, mu_ref, *o_refs):
    g = g_ref[...]
    sc = 1 + sc_ref[0]
    sh = sh_ref[0]
    h = _rms(x_ref[...], g) * sc + sh
    hp = (_rms(xp_ref[...], g) * sc + sh)[7:8, :]
    first_tile = (pl.program_id(0) % per) == 0
    hp = jnp.where(first_tile, 0.0, hp)
    row = lax.broadcasted_iota(jnp.int32, h.shape, 0)
    hs = jnp.where(row == 0, hp, pltpu.roll(h, 1, axis=0))
    xx = hs - h
    for j, o_ref in enumerate(o_refs):
        o_ref[...] = (h + xx * mu_ref[j:j + 1, :]).astype(o_ref.dtype)


def premix(x, g, sh, sc, mu, rows_per_batch, tm=256):
    M, D = x.shape
    per = rows_per_batch // tm
    sub = 8
    n_mix = mu.shape[0]
    return pl.pallas_call(
        functools.partial(_premix_kernel, per),
        out_shape=[jax.ShapeDtypeStruct((M, D), BF16)] * n_mix,
        grid=(M // tm,),
        in_specs=[
            pl.BlockSpec((tm, D), lambda i: (i, 0)),
            pl.BlockSpec((sub, D), lambda i: (jnp.maximum(i * (tm // sub) - 1, 0), 0)),
            pl.BlockSpec((1, D), lambda i: (0, 0)),
            pl.BlockSpec((1, 1, D), lambda i: (i // per, 0, 0)),
            pl.BlockSpec((1, 1, D), lambda i: (i // per, 0, 0)),
            pl.BlockSpec((n_mix, D), lambda i: (0, 0)),
        ],
        out_specs=[pl.BlockSpec((tm, D), lambda i: (i, 0))] * n_mix,
        compiler_params=_cparams(("arbitrary",)),
        name="premix",
    )(x, x, g.reshape(1, D), sh, sc, mu)


def _mm_kernel(epilogue, nk, a_ref, w_ref, *rest):
    if epilogue == "resid":
        x_ref, gt_ref, o_ref, acc_ref = rest
    else:
        o_ref, acc_ref = rest
    k = pl.program_id(2)
    w = w_ref[0] if len(w_ref.shape) == 3 else w_ref[...]
    part = jnp.dot(a_ref[...], w, preferred_element_type=F32)

    def finish(acc):
        if epilogue == "resid":
            o_ref[...] = x_ref[...] + gt_ref[0] * acc
        elif epilogue == "relu2":
            o_ref[...] = jnp.square(jnp.maximum(acc, 0.0)).astype(o_ref.dtype)
        else:
            o_ref[...] = acc.astype(o_ref.dtype)

    if nk == 1:
        finish(part)
    else:
        @pl.when(k == 0)
        def _():
            acc_ref[...] = part

        @pl.when(jnp.logical_and(k > 0, k < nk - 1))
        def _():
            acc_ref[...] += part

        @pl.when(k == nk - 1)
        def _():
            finish(acc_ref[...] + part)


def matmul(a, w, *, layer=None, epilogue="none", out_dtype=F32, resid=None, gate=None,
           rows_per_batch=None, tm=1024, tn=1024, tk=2048):
    M, K = a.shape
    N = w.shape[-1]
    tk = min(tk, K)
    tn = min(tn, N)
    nk = K // tk
    if layer is None:
        w_spec = pl.BlockSpec((tk, tn), lambda i, j, k: (k, j))
    else:
        w_spec = pl.BlockSpec((1, tk, tn), lambda i, j, k: (layer, k, j))
    in_specs = [pl.BlockSpec((tm, tk), lambda i, j, k: (i, k)), w_spec]
    args = [a, w]
    if epilogue == "resid":
        per = rows_per_batch // tm
        in_specs += [
            pl.BlockSpec((tm, tn), lambda i, j, k: (i, j)),
            pl.BlockSpec((1, 1, tn), lambda i, j, k: (i // per, 0, j)),
        ]
        args += [resid, gate]
    acc_shape = (tm, tn) if nk > 1 else (8, LANES)
    return pl.pallas_call(
        functools.partial(_mm_kernel, epilogue, nk),
        out_shape=jax.ShapeDtypeStruct((M, N), out_dtype),
        grid=(M // tm, N // tn, nk),
        in_specs=in_specs,
        out_specs=pl.BlockSpec((tm, tn), lambda i, j, k: (i, j)),
        scratch_shapes=[pltpu.VMEM(acc_shape, F32)],
        compiler_params=_cparams(("arbitrary", "arbitrary", "arbitrary")),
        name="mm_" + epilogue,
    )(*args)


def _lora_kernel(act, a_ref, w1_ref, w2_ref, b_ref, o_ref):
    t = jnp.dot(a_ref[...], w1_ref[...], preferred_element_type=F32)
    if act == "tanh":
        t = jnp.tanh(t)
    elif act == "sigmoid":
        t = jax.nn.sigmoid(t)
    o_ref[...] = jnp.dot(t.astype(BF16), w2_ref[...], preferred_element_type=F32) + b_ref[...]


def lora(a, w1, w2, bias, act, tm=512):
    M, D = a.shape
    R = w1.shape[1]
    N = w2.shape[1]
    return pl.pallas_call(
        functools.partial(_lora_kernel, act),
        out_shape=jax.ShapeDtypeStruct((M, N), F32),
        grid=(M // tm,),
        in_specs=[
            pl.BlockSpec((tm, D), lambda i: (i, 0)),
            pl.BlockSpec((D, R), lambda i: (0, 0)),
            pl.BlockSpec((R, N), lambda i: (0, 0)),
            pl.BlockSpec((1, N), lambda i: (0, 0)),
        ],
        out_specs=pl.BlockSpec((tm, N), lambda i: (i, 0)),
        compiler_params=_cparams(("arbitrary",)),
        name="lora_" + act,
    )(a, w1, w2, bias.reshape(1, N))


def _pad_rank(w1, w2):
    r = w1.shape[1]
    rp = -(-r // LANES) * LANES
    if rp != r:
        w1 = jnp.pad(w1, ((0, 0), (0, rp - r)))
        w2 = jnp.pad(w2, ((0, rp - r), (0, 0)))
    return w1.astype(BF16), w2.astype(BF16)


def _transpose_lane_groups(xs):
    nl = xs[0].shape[-1]
    n = len(xs)
    width = nl // n
    grp = lax.broadcasted_iota(jnp.int32, xs[0].shape, 1) // width
    ys = []
    for c in range(n):
        acc = None
        for b in range(n):
            shift = ((b - c) * width) % nl
            piece = pltpu.roll(xs[b], shift, axis=1) if shift else xs[b]
            acc = piece if acc is None else jnp.where(grp == b, piece, acc)
        ys.append(acc)
    return ys


def _rows_to_head_lanes(x, nb, o_ref):
    tq = x.shape[0] // nb
    ng = x.shape[1] // LANES
    for g in range(ng):
        xs = [x[b * tq:(b + 1) * tq, g * LANES:(g + 1) * LANES] for b in range(nb)]
        for c, y in enumerate(_transpose_lane_groups(xs)):
            o_ref[c * ng + g] = y


def _head_lanes_to_rows(y_ref, nb):
    ng = y_ref.shape[0] // nb
    cols = [_transpose_lane_groups([y_ref[c * ng + g] for c in range(nb)]) for g in range(ng)]
    return [jnp.concatenate([cols[g][b] for g in range(ng)], axis=1) for b in range(nb)]


def _proj_hl_kernel(a_ref, w_ref, o_ref):
    nb, tq, kdim = a_ref.shape
    acc = jnp.dot(a_ref[...].reshape(nb * tq, kdim), w_ref[...], preferred_element_type=F32)
    _rows_to_head_lanes(acc, nb, o_ref)


def proj_head_lanes(a, w, tq=128):
    B, T, K = a.shape
    D = w.shape[1]
    return pl.pallas_call(
        _proj_hl_kernel,
        out_shape=jax.ShapeDtypeStruct((HEAD_DIM, T, LANES), F32),
        grid=(T // tq,),
        in_specs=[
            pl.BlockSpec((B, tq, K), lambda i: (0, i, 0)),
            pl.BlockSpec((K, D), lambda i: (0, 0)),
        ],
        out_specs=pl.BlockSpec((HEAD_DIM, tq, LANES), lambda i: (0, i, 0)),
        compiler_params=_cparams(("arbitrary",)),
        name="proj_head_lanes",
    )(a, w)


def _lora_hl_kernel(act, a_ref, w1_ref, w2_ref, b_ref, o_ref):
    nb, tq, kdim = a_ref.shape
    t = jnp.dot(a_ref[...].reshape(nb * tq, kdim), w1_ref[...], preferred_element_type=F32)
    if act == "tanh":
        t = jnp.tanh(t)
    o = jnp.dot(t.astype(BF16), w2_ref[...], preferred_element_type=F32) + b_ref[...]
    _rows_to_head_lanes(o, nb, o_ref)


def lora_head_lanes(a, w1, w2, bias, act, tq=128):
    B, T, D = a.shape
    R = w1.shape[1]
    N = w2.shape[1]
    return pl.pallas_call(
        functools.partial(_lora_hl_kernel, act),
        out_shape=jax.ShapeDtypeStruct((HEAD_DIM, T, LANES), F32),
        grid=(T // tq,),
        in_specs=[
            pl.BlockSpec((B, tq, D), lambda i: (0, i, 0)),
            pl.BlockSpec((D, R), lambda i: (0, 0)),
            pl.BlockSpec((R, N), lambda i: (0, 0)),
            pl.BlockSpec((1, N), lambda i: (0, 0)),
        ],
        out_specs=pl.BlockSpec((HEAD_DIM, tq, LANES), lambda i: (0, i, 0)),
        compiler_params=_cparams(("arbitrary",)),
        name="lora_hl_" + act,
    )(a, w1, w2, bias.reshape(1, N))


def _out_proj_hl_kernel(y_ref, g_ref, w_ref, x_ref, gt_ref, o_ref):
    nb, tq, d = g_ref.shape
    ys = _head_lanes_to_rows(y_ref, nb)
    a = jnp.concatenate([(ys[b] * g_ref[b]).astype(BF16) for b in range(nb)], axis=0)
    acc = jnp.dot(a, w_ref[...], preferred_element_type=F32)
    o_ref[...] = x_ref[...] + gt_ref[...] * acc.reshape(nb, tq, acc.shape[1])


def out_proj_head_lanes(y, g, w, x, gate, tq=64):
    B, T, D = g.shape
    N = w.shape[1]
    return pl.pallas_call(
        _out_proj_hl_kernel,
        out_shape=jax.ShapeDtypeStruct((B, T, N), F32),
        grid=(T // tq,),
        in_specs=[
            pl.BlockSpec((HEAD_DIM, tq, LANES), lambda i: (0, i, 0)),
            pl.BlockSpec((B, tq, D), lambda i: (0, i, 0)),
            pl.BlockSpec((D, N), lambda i: (0, 0)),
            pl.BlockSpec((B, tq, N), lambda i: (0, i, 0)),
            pl.BlockSpec((B, 1, N), lambda i: (0, 0, 0)),
        ],
        out_specs=pl.BlockSpec((B, tq, N), lambda i: (0, i, 0)),
        compiler_params=_cparams(("arbitrary",)),
        name="out_proj_head_lanes",
    )(y, g, w, x, gate)


def _scan_kernel(r_ref, wl_ref, k_ref, v_ref, al_ref, kk_ref, ka_ref, rk_ref,
                 lnw_ref, lnb_ref, o_ref,
                 s_ref, v_s, dec_s, kap_s, nka_s, km_s, yt_s):
    dh, tb, nl = r_ref.shape
    half = dh // 2

    @pl.when(pl.program_id(0) == 0)
    def _():
        s_ref[...] = jnp.zeros_like(s_ref)

    def seq(ref):
        return ref[...].reshape(dh, tb, nl)

    def put(ref, val):
        ref[...] = val.reshape(dh * tb, nl)

    put(v_s, v_ref[...])
    z = -wl_ref[...]
    softplus = jnp.maximum(z, 0.0) + jnp.log1p(jnp.exp(-jnp.abs(z)))
    put(dec_s, jnp.exp(-jnp.exp(-softplus - 0.5)))
    a = jax.nn.sigmoid(al_ref[...])
    k = k_ref[...]
    kk = k * kk_ref[...]
    nrm = jnp.sqrt(jnp.sum(kk * kk, axis=0, keepdims=True))
    kap = kk / jnp.maximum(nrm, 1e-12)
    put(kap_s, kap)
    put(nka_s, -(kap * a))
    put(km_s, k * (1 + (a - 1) * ka_ref[...]))

    def row(ref, t, c):
        return jnp.broadcast_to(ref[pl.ds(c * tb + t, 1), :], (half, nl))

    def r_row(t, c):
        return jnp.broadcast_to(r_ref[c, pl.ds(t, 1), :], (half, nl))

    def tile(hf, t):
        return pl.ds(hf * half * tb + t, half, stride=tb)

    zeros = jnp.zeros((half, nl), F32)

    def first_sa(hf):
        def body(g, acc):
            for j in range(SCAN_KU):
                c = g * SCAN_KU + j
                acc = acc + s_ref[hf, c] * row(kap_s, 0, c)
            return acc
        return lax.fori_loop(0, dh // SCAN_KU, body, zeros)

    def run_half(hf, t, t_next, sa):
        v_t = v_s[tile(hf, t), :]

        def body(g, carry):
            y, sa_next = carry
            for j in range(SCAN_KU):
                c = g * SCAN_KU + j
                s_new = (s_ref[hf, c] * row(dec_s, t, c) + sa * row(nka_s, t, c)
                         + v_t * row(km_s, t, c))
                s_ref[hf, c] = s_new
                y = y + s_new * r_row(t, c)
                sa_next = sa_next + s_new * row(kap_s, t_next, c)
            return y, sa_next

        y, sa_next = lax.fori_loop(0, dh // SCAN_KU, body, (zeros, zeros))
        yt_s[pl.ds(pl.multiple_of(t * dh + hf * half, half), half), :] = y
        return sa_next

    def step(t, sa):
        t_next = jnp.minimum(t + 1, tb - 1)
        return run_half(0, t, t_next, sa[0]), run_half(1, t, t_next, sa[1])

    lax.fori_loop(0, tb, step, (first_sa(0), first_sa(1)))

    for ch in range(dh):
        o_ref[ch] = yt_s[pl.ds(ch, tb, stride=dh), :]
    y = o_ref[...]
    mean = jnp.mean(y, axis=0, keepdims=True)
    d = y - mean
    var = jnp.mean(d * d, axis=0, keepdims=True)
    yn = d * lax.rsqrt(var + GN_EPS) * lnw_ref[...] + lnb_ref[...]
    bonus = jnp.sum(r_ref[...] * seq(km_s) * rk_ref[...], axis=0, keepdims=True)
    o_ref[...] = yn + bonus * v_ref[...]


def rwkv_scan(r, wl, k, v, al, k_k, k_a, r_k, ln_w, ln_b):
    dh, T, nl = r.shape
    tb = SCAN_TB
    seq = pl.BlockSpec((dh, tb, nl), lambda i: (0, i, 0))
    par = pl.BlockSpec((dh, 1, nl), lambda i: (0, 0, 0))
    return pl.pallas_call(
        _scan_kernel,
        out_shape=jax.ShapeDtypeStruct((dh, T, nl), F32),
        grid=(T // tb,),
        in_specs=[seq] * 5 + [par] * 5,
        out_specs=seq,
        scratch_shapes=([pltpu.VMEM((2, dh, dh // 2, nl), F32)]
                        + [pltpu.VMEM((dh * tb, nl), F32)] * 6),
        compiler_params=_cparams(("arbitrary",)),
        name="rwkv_scan",
    )(r, wl, k, v, al, k_k, k_a, r_k, ln_w, ln_b)


def _bias_kernel(g_ref, o_ref):
    nq, nw = o_ref.shape[1], o_ref.shape[2]
    ng = g_ref.shape[2]
    row = lax.broadcasted_iota(jnp.int32, (nq, ng), 0)
    x = jnp.broadcast_to(g_ref[0], (nq, ng))
    shift = 1
    while shift < nq:
        x = jnp.where((row & shift) != 0, pltpu.roll(x, shift, axis=1), x)
        shift *= 2
    qi = lax.broadcasted_iota(jnp.int32, (nq, nw), 0) // CHUNK
    kj = lax.broadcasted_iota(jnp.int32, (nq, nw), 1) // CHUNK
    band = jnp.logical_and(kj >= qi, kj <= qi + N_LEFT_CHUNKS)
    o_ref[0] = jnp.where(band, x[:, :nw], MASK_VALUE)


def band_bias(rel_bias):
    H = rel_bias.shape[0]
    ng = -(-(ATTN_W + ATTN_Q) // LANES) * LANES
    far = rel_bias[:, 2 * REL_CLIP:]
    near = rel_bias[:, 2 * REL_CLIP - 1:REL_CLIP + ATTN_PAD - ATTN_W:-1]
    n_far = ATTN_W - near.shape[1]
    g = jnp.concatenate(
        [jnp.broadcast_to(far, (H, n_far)), near, jnp.broadcast_to(far, (H, ng - ATTN_W))], axis=1)
    return pl.pallas_call(
        _bias_kernel,
        out_shape=jax.ShapeDtypeStruct((H, ATTN_Q, ATTN_W), F32),
        grid=(H,),
        in_specs=[pl.BlockSpec((1, 1, ng), lambda h: (h, 0, 0))],
        out_specs=pl.BlockSpec((1, ATTN_Q, ATTN_W), lambda h: (h, 0, 0)),
        compiler_params=_cparams(("arbitrary",)),
        name="band_bias",
    )(g.reshape(H, 1, ng))


def _attn_kernel(q_ref, k_ref, v_ref, b_ref, o_ref):
    T = q_ref.shape[1]
    n_steps = T // ATTN_Q
    n_masked = -(-ATTN_PAD // ATTN_Q)
    lane = lax.broadcasted_iota(jnp.int32, (ATTN_Q, LANES), 1)
    scale = HEAD_DIM ** -0.5
    head_scale = [jnp.where(lane < HEAD_DIM, scale, 0.0).astype(BF16),
                  jnp.where(lane >= HEAD_DIM, scale, 0.0).astype(BF16)]
    kpos = lax.broadcasted_iota(jnp.int32, (2 * ATTN_Q, ATTN_W), 1)

    def step(n, masked):
        start = pl.multiple_of(n * ATTN_Q, ATTN_Q)
        q2 = q_ref[0, pl.ds(start, ATTN_Q), :]
        kw = k_ref[0, pl.ds(start, ATTN_W), :]
        vw = v_ref[0, pl.ds(start, ATTN_W), :]
        qs = jnp.concatenate([q2 * head_scale[0], q2 * head_scale[1]], axis=0)
        s = lax.dot_general(qs, kw, (((1,), (1,)), ((), ())), preferred_element_type=F32)
        s = s + b_ref[...].reshape(2 * ATTN_Q, ATTN_W)
        if masked:
            s = jnp.where(kpos + start >= ATTN_PAD, s, MASK_VALUE)
        m = jnp.max(s, axis=-1, keepdims=True)
        p = jnp.exp(s - m)
        l = jnp.sum(p, axis=-1, keepdims=True)
        o = jnp.dot(p.astype(BF16), vw, preferred_element_type=F32) / l
        o_ref[0, pl.ds(start, ATTN_Q), :] = jnp.where(
            lane < HEAD_DIM, o[:ATTN_Q], o[ATTN_Q:]).astype(o_ref.dtype)

    def masked_body(n, carry):
        step(n, True)
        return carry

    def body(n, carry):
        step(n, False)
        return carry

    lax.fori_loop(0, n_masked, masked_body, 0, unroll=2)
    lax.fori_loop(n_masked, n_steps, body, 0, unroll=4)


def band_attention(q, kvp, bias):
    B, T, D = q.shape
    Tp = kvp.shape[1]
    n_pairs = D // LANES
    return pl.pallas_call(
        _attn_kernel,
        out_shape=jax.ShapeDtypeStruct((B, T, D), BF16),
        grid=(B, n_pairs),
        in_specs=[
            pl.BlockSpec((1, T, LANES), lambda b, p: (b, 0, p)),
            pl.BlockSpec((1, Tp, LANES), lambda b, p: (b, 0, p)),
            pl.BlockSpec((1, Tp, LANES), lambda b, p: (b, 0, p + n_pairs)),
            pl.BlockSpec((2, ATTN_Q, ATTN_W), lambda b, p: (p, 0, 0)),
        ],
        out_specs=pl.BlockSpec((1, T, LANES), lambda b, p: (b, 0, p)),
        compiler_params=_cparams(("arbitrary", "arbitrary")),
        name="band_attention",
    )(q, kvp, kvp, bias)


def _group_cols(w, H, group):
    lead = w.shape[:-1]
    w = w.reshape(lead + (H, HEAD_DIM // group, group))
    w = jnp.moveaxis(w, -3, -1)
    return w.reshape(lead + (H * HEAD_DIM,))


def _param_head_lanes(p, B, H, group):
    t = jnp.transpose(p.astype(F32).reshape(H, HEAD_DIM // group, group), (2, 1, 0))
    return jnp.tile(t.reshape(HEAD_DIM, H), (1, B)).reshape(HEAD_DIM, 1, B * H)


def kernel(x, c, w_ada, b_ada, g_mix, g_mlp, w_up, w_down, rwkv_mu, rwkv_w_r, rwkv_w_k, rwkv_w_v, rwkv_w_o, rwkv_w0, rwkv_w1, rwkv_w2, rwkv_a0, rwkv_a1, rwkv_a2, rwkv_g1, rwkv_g2, rwkv_k_k, rwkv_k_a, rwkv_r_k, rwkv_ln_w, rwkv_ln_b, attn_w_q, attn_w_o, attn_rel_bias, w_ada_kv, b_ada_kv, g_kv, w_k_shared, w_v_shared, g_final):
    B, T, D = x.shape
    H = D // HEAD_DIM
    M = B * T
    depth = w_ada.shape[0]
    n_a = rwkv_mu.shape[0]
    group = LANES // H
    assert T % ATTN_Q == 0 and T % SCAN_TB == 0 and B * H == LANES and group == B

    xf = x.reshape(M, D)
    c_pad = jnp.pad(c, ((0, 16 - B), (0, 0))).astype(BF16)
    mods = ada_mod(c_pad, w_ada, b_ada)[:, :B]
    mod_kv = ada_mod(c_pad, w_ada_kv[None], b_ada_kv[None])[0, :B]

    def split(m, n):
        return [p.reshape(B, 1, D) for p in jnp.split(m, n, axis=-1)]

    w_up_bf = w_up.astype(BF16)
    w_down_bf = w_down.astype(BF16)
    kvp = None
    for layer in range(depth):
        sh1, sc1, gt1, sh2, sc2, gt2 = split(mods[layer], 6)
        if layer == n_a:
            sh_kv, sc_kv = split(mod_kv, 2)
            h_kv = norm_mod(xf, g_kv, sh_kv, sc_kv, T)
            w_kv = jnp.concatenate([w_k_shared, w_v_shared], axis=1).astype(BF16)
            kv = matmul(h_kv, w_kv, out_dtype=BF16).reshape(B, T, 2 * D)
            kvp = jnp.pad(kv, ((0, 0), (ATTN_PAD, 0), (0, 0)))
        if layer < n_a:
            i = layer
            gc = functools.partial(_group_cols, H=H, group=group)
            hp = functools.partial(_param_head_lanes, B=B, H=H, group=group)
            xr, xw, xk, xv, xa, xg = premix(xf, g_mix[layer], sh1, sc1, rwkv_mu[i], T)
            rows = lambda a: a.reshape(B, T, D)
            r = proj_head_lanes(rows(xr), gc(rwkv_w_r[i]).astype(BF16))
            k = proj_head_lanes(rows(xk), gc(rwkv_w_k[i]).astype(BF16))
            v = proj_head_lanes(rows(xv), gc(rwkv_w_v[i]).astype(BF16))
            wl = lora_head_lanes(rows(xw), *_pad_rank(rwkv_w1[i], gc(rwkv_w2[i])),
                                 gc(rwkv_w0[i]), "tanh")
            al = lora_head_lanes(rows(xa), *_pad_rank(rwkv_a1[i], gc(rwkv_a2[i])),
                                 gc(rwkv_a0[i]), "none")
            g = lora(xg, *_pad_rank(rwkv_g1[i], gc(rwkv_g2[i])), jnp.zeros((D,), F32), "sigmoid")
            y = rwkv_scan(r, wl, k, v, al,
                          hp(rwkv_k_k[i]), hp(rwkv_k_a[i]), hp(rwkv_r_k[i].reshape(D)),
                          hp(rwkv_ln_w[i]), hp(rwkv_ln_b[i]))
            w_o = gc(rwkv_w_o[i].T).T
            xf = out_proj_head_lanes(y, rows(g), w_o.astype(BF16), rows(xf), gt1).reshape(M, D)
        else:
            i = layer - n_a
            h = norm_mod(xf, g_mix[layer], sh1, sc1, T)
            q = matmul(h, attn_w_q[i].astype(BF16), out_dtype=BF16, **SQUARE_TILES).reshape(B, T, D)
            bias = band_bias(attn_rel_bias[i].astype(F32))
            yg = band_attention(q, kvp, bias).reshape(M, D)
            xf = matmul(yg, attn_w_o[i].astype(BF16), epilogue="resid", resid=xf, gate=gt1,
                        rows_per_batch=T, **SQUARE_TILES)
        h = norm_mod(xf, g_mlp[layer], sh2, sc2, T)
        hid = matmul(h, w_up_bf, layer=layer, epilogue="relu2", out_dtype=BF16)
        xf = matmul(hid, w_down_bf, layer=layer, epilogue="resid", resid=xf, gate=gt2,
                    rows_per_batch=T)
    return final_norm(xf, g_final).reshape(B, T, D)
```

```python
import functools

import jax
import jax.numpy as jnp
from jax import lax
from jax.experimental import pallas as pl
from jax.experimental.pallas import tpu as pltpu

F32 = jnp.float32
BF16 = jnp.bfloat16

CHUNK = 64
N_LEFT_CHUNKS = 8
HEAD_DIM = 64
REL_CLIP = 256
RMS_EPS = 1e-6
GN_EPS = 64e-5
MASK_VALUE = -1e30

LANES = 128
ATTN_Q = 2 * CHUNK
ATTN_PAD = N_LEFT_CHUNKS * CHUNK
ATTN_W = ATTN_PAD + ATTN_Q
SCAN_TB = 32
SCAN_KU = 64
VMEM_LIMIT = 48 * 1024 * 1024
SQUARE_TILES = dict(tm=512, tn=2048)


def _cparams(sem):
    return pltpu.CompilerParams(dimension_semantics=sem, vmem_limit_bytes=VMEM_LIMIT)


def _ada_kernel(c_ref, w_ref, b_ref, o_ref):
    w = w_ref[0].astype(BF16)
    o_ref[0] = jnp.dot(c_ref[...], w, preferred_element_type=F32) + b_ref[0]


def ada_mod(c_pad, w, b, tn=1024):
    L, D, N = w.shape
    R = c_pad.shape[0]
    return pl.pallas_call(
        _ada_kernel,
        out_shape=jax.ShapeDtypeStruct((L, R, N), F32),
        grid=(L, N // tn),
        in_specs=[
            pl.BlockSpec((R, D), lambda l, j: (0, 0)),
            pl.BlockSpec((1, D, tn), lambda l, j: (l, 0, j)),
            pl.BlockSpec((1, 1, tn), lambda l, j: (l, 0, j)),
        ],
        out_specs=pl.BlockSpec((1, R, tn), lambda l, j: (l, 0, j)),
        compiler_params=_cparams(("arbitrary", "arbitrary")),
        name="ada_mod",
    )(c_pad, w, b.reshape(L, 1, N))


def _rms(x, g):
    ms = jnp.mean(x * x, axis=-1, keepdims=True)
    return x * lax.rsqrt(ms + RMS_EPS) * g


def _norm_mod_kernel(x_ref, g_ref, sh_ref, sc_ref, o_ref):
    y = _rms(x_ref[...], g_ref[...])
    o_ref[...] = (y * (1 + sc_ref[0]) + sh_ref[0]).astype(o_ref.dtype)


def norm_mod(x, g, sh, sc, rows_per_batch, tm=512):
    M, D = x.shape
    per = rows_per_batch // tm
    return pl.pallas_call(
        _norm_mod_kernel,
        out_shape=jax.ShapeDtypeStruct((M, D), BF16),
        grid=(M // tm,),
        in_specs=[
            pl.BlockSpec((tm, D), lambda i: (i, 0)),
            pl.BlockSpec((1, D), lambda i: (0, 0)),
            pl.BlockSpec((1, 1, D), lambda i: (i // per, 0, 0)),
            pl.BlockSpec((1, 1, D), lambda i: (i // per, 0, 0)),
        ],
        out_specs=pl.BlockSpec((tm, D), lambda i: (i, 0)),
        compiler_params=_cparams(("arbitrary",)),
        name="norm_mod",
    )(x, g.reshape(1, D), sh, sc)


def _final_norm_kernel(x_ref, g_ref, o_ref):
    o_ref[...] = _rms(x_ref[...], g_ref[...])


def final_norm(x, g, tm=512):
    M, D = x.shape
    return pl.pallas_call(
        _final_norm_kernel,
        out_shape=jax.ShapeDtypeStruct((M, D), F32),
        grid=(M // tm,),
        in_specs=[
            pl.BlockSpec((tm, D), lambda i: (i, 0)),
            pl.BlockSpec((1, D), lambda i: (0, 0)),
        ],
        out_specs=pl.BlockSpec((tm, D), lambda i: (i, 0)),
        compiler_params=_cparams(("arbitrary",)),
        name="final_norm",
    )(x, g.reshape(1, D))


def _premix_kernel(per, x_ref, xp_ref, g_ref, sh_ref, sc_ref, mu_ref, *o_refs):
    g = g_ref[...]
    sc = 1 + sc_ref[0]
    sh = sh_ref[0]
    h = _rms(x_ref[...], g) * sc + sh
    hp = (_rms(xp_ref[...], g) * sc + sh)[7:8, :]
    first_tile = (pl.program_id(0) % per) == 0
    hp = jnp.where(first_tile, 0.0, hp)
    row = lax.broadcasted_iota(jnp.int32, h.shape, 0)
    hs = jnp.where(row == 0, hp, pltpu.roll(h, 1, axis=0))
    xx = hs - h
    for j, o_ref in enumerate(o_refs):
        o_ref[...] = (h + xx * mu_ref[j:j + 1, :]).astype(o_ref.dtype)


def premix(x, g, sh, sc, mu, rows_per_batch, tm=256):
    M, D = x.shape
    per = rows_per_batch // tm
    sub = 8
    n_mix = mu.shape[0]
    return pl.pallas_call(
        functools.partial(_premix_kernel, per),
        out_shape=[jax.ShapeDtypeStruct((M, D), BF16)] * n_mix,
        grid=(M // tm,),
        in_specs=[
            pl.BlockSpec((tm, D), lambda i: (i, 0)),
            pl.BlockSpec((sub, D), lambda i: (jnp.maximum(i * (tm // sub) - 1, 0), 0)),
            pl.BlockSpec((1, D), lambda i: (0, 0)),
            pl.BlockSpec((1, 1, D), lambda i: (i // per, 0, 0)),
            pl.BlockSpec((1, 1, D), lambda i: (i // per, 0, 0)),
            pl.BlockSpec((n_mix, D), lambda i: (0, 0)),
        ],
        out_specs=[pl.BlockSpec((tm, D), lambda i: (i, 0))] * n_mix,
        compiler_params=_cparams(("arbitrary",)),
        name="premix",
    )(x, x, g.reshape(1, D), sh, sc, mu)


def _mm_kernel(epilogue, nk, a_ref, w_ref, *rest):
    if epilogue == "resid":
        x_ref, gt_ref, o_ref, acc_ref = rest
    else:
        o_ref, acc_ref = rest
    k = pl.program_id(2)
    w = w_ref[0] if len(w_ref.shape) == 3 else w_ref[...]
    part = jnp.dot(a_ref[...], w, preferred_element_type=F32)

    def finish(acc):
        if epilogue == "resid":
            o_ref[...] = x_ref[...] + gt_ref[0] * acc
        elif epilogue == "relu2":
            o_ref[...] = jnp.square(jnp.maximum(acc, 0.0)).astype(o_ref.dtype)
        else:
            o_ref[...] = acc.astype(o_ref.dtype)

    if nk == 1:
        finish(part)
    else:
        @pl.when(k == 0)
        def _():
            acc_ref[...] = part

        @pl.when(jnp.logical_and(k > 0, k < nk - 1))
        def _():
            acc_ref[...] += part

        @pl.when(k == nk - 1)
        def _():
            finish(acc_ref[...] + part)


def matmul(a, w, *, layer=None, epilogue="none", out_dtype=F32, resid=None, gate=None,
           rows_per_batch=None, tm=1024, tn=1024, tk=2048):
    M, K = a.shape
    N = w.shape[-1]
    tk = min(tk, K)
    tn = min(tn, N)
    nk = K // tk
    if layer is None:
        w_spec = pl.BlockSpec((tk, tn), lambda i, j, k: (k, j))
    else:
        w_spec = pl.BlockSpec((1, tk, tn), lambda i, j, k: (layer, k, j))
    in_specs = [pl.BlockSpec((tm, tk), lambda i, j, k: (i, k)), w_spec]
    args = [a, w]
    if epilogue == "resid":
        per = rows_per_batch // tm
        in_specs += [
            pl.BlockSpec((tm, tn), lambda i, j, k: (i, j)),
            pl.BlockSpec((1, 1, tn), lambda i, j, k: (i // per, 0, j)),
        ]
        args += [resid, gate]
    acc_shape = (tm, tn) if nk > 1 else (8, LANES)
    return pl.pallas_call(
        functools.partial(_mm_kernel, epilogue, nk),
        out_shape=jax.ShapeDtypeStruct((M, N), out_dtype),
        grid=(M // tm, N // tn, nk),
        in_specs=in_specs,
        out_specs=pl.BlockSpec((tm, tn), lambda i, j, k: (i, j)),
        scratch_shapes=[pltpu.VMEM(acc_shape, F32)],
        compiler_params=_cparams(("arbitrary", "arbitrary", "arbitrary")),
        name="mm_" + epilogue,
    )(*args)


def _pad_rank(w1, w2):
    r = w1.shape[1]
    rp = -(-r // LANES) * LANES
    if rp != r:
        w1 = jnp.pad(w1, ((0, 0), (0, rp - r)))
        w2 = jnp.pad(w2, ((0, rp - r), (0, 0)))
    return w1.astype(BF16), w2.astype(BF16)


def _transpose_lane_groups(xs):
    nl = xs[0].shape[-1]
    n = len(xs)
    width = nl // n
    grp = lax.broadcasted_iota(jnp.int32, xs[0].shape, 1) // width
    ys = []
    for c in range(n):
        acc = None
        for b in range(n):
            shift = ((b - c) * width) % nl
            piece = pltpu.roll(xs[b], shift, axis=1) if shift else xs[b]
            acc = piece if acc is None else jnp.where(grp == b, piece, acc)
        ys.append(acc)
    return ys


def _rows_to_head_lanes(x, nb, o_ref):
    tq = x.shape[0] // nb
    ng = x.shape[1] // LANES
    for g in range(ng):
        xs = [x[b * tq:(b + 1) * tq, g * LANES:(g + 1) * LANES] for b in range(nb)]
        for c, y in enumerate(_transpose_lane_groups(xs)):
            o_ref[c * ng + g] = y


def _head_lanes_to_rows(y_ref, nb):
    ng = y_ref.shape[0] // nb
    cols = [_transpose_lane_groups([y_ref[c * ng + g] for c in range(nb)]) for g in range(ng)]
    return [jnp.concatenate([cols[g][b] for g in range(ng)], axis=1) for b in range(nb)]


def _proj_lora_kernel(act, lora_hl, a_ref, w_ref, al_ref, w1_ref, w2_ref, b_ref, o_ref, ol_ref):
    nb, tq, kdim = a_ref.shape
    acc = jnp.dot(a_ref[...].reshape(nb * tq, kdim), w_ref[...], preferred_element_type=F32)
    _rows_to_head_lanes(acc, nb, o_ref)
    t = jnp.dot(al_ref[...].reshape(nb * tq, kdim), w1_ref[...], preferred_element_type=F32)
    if act == "tanh":
        t = jnp.tanh(t)
    elif act == "sigmoid":
        t = jax.nn.sigmoid(t)
    o = jnp.dot(t.astype(BF16), w2_ref[...], preferred_element_type=F32) + b_ref[...]
    if lora_hl:
        _rows_to_head_lanes(o, nb, ol_ref)
    else:
        ol_ref[...] = o.reshape(nb, tq, o.shape[1])


def proj_lora_head_lanes(a, w, a_lora, w1, w2, bias, act, lora_hl, tq=128):
    B, T, K = a.shape
    D = w.shape[1]
    R = w1.shape[1]
    hl = jax.ShapeDtypeStruct((HEAD_DIM, T, LANES), F32)
    hl_spec = pl.BlockSpec((HEAD_DIM, tq, LANES), lambda i: (0, i, 0))
    row_spec = pl.BlockSpec((B, tq, D), lambda i: (0, i, 0))
    return pl.pallas_call(
        functools.partial(_proj_lora_kernel, act, lora_hl),
        out_shape=[hl, hl if lora_hl else jax.ShapeDtypeStruct((B, T, D), F32)],
        grid=(T // tq,),
        in_specs=[
            pl.BlockSpec((B, tq, K), lambda i: (0, i, 0)),
            pl.BlockSpec((K, D), lambda i: (0, 0)),
            pl.BlockSpec((B, tq, K), lambda i: (0, i, 0)),
            pl.BlockSpec((K, R), lambda i: (0, 0)),
            pl.BlockSpec((R, D), lambda i: (0, 0)),
            pl.BlockSpec((1, D), lambda i: (0, 0)),
        ],
        out_specs=[hl_spec, hl_spec if lora_hl else row_spec],
        compiler_params=_cparams(("arbitrary",)),
        name="proj_lora_" + act,
    )(a, w, a_lora, w1, w2, bias.reshape(1, D))


def _out_proj_hl_kernel(y_ref, g_ref, w_ref, x_ref, gt_ref, o_ref):
    nb, tq, d = g_ref.shape
    ys = _head_lanes_to_rows(y_ref, nb)
    a = jnp.concatenate([(ys[b] * g_ref[b]).astype(BF16) for b in range(nb)], axis=0)
    acc = jnp.dot(a, w_ref[...], preferred_element_type=F32)
    o_ref[...] = x_ref[...] + gt_ref[...] * acc.reshape(nb, tq, acc.shape[1])


def out_proj_head_lanes(y, g, w, x, gate, tq=64):
    B, T, D = g.shape
    N = w.shape[1]
    return pl.pallas_call(
        _out_proj_hl_kernel,
        out_shape=jax.ShapeDtypeStruct((B, T, N), F32),
        grid=(T // tq,),
        in_specs=[
            pl.BlockSpec((HEAD_DIM, tq, LANES), lambda i: (0, i, 0)),
            pl.BlockSpec((B, tq, D), lambda i: (0, i, 0)),
            pl.BlockSpec((D, N), lambda i: (0, 0)),
            pl.BlockSpec((B, tq, N), lambda i: (0, i, 0)),
            pl.BlockSpec((B, 1, N), lambda i: (0, 0, 0)),
        ],
        out_specs=pl.BlockSpec((B, tq, N), lambda i: (0, i, 0)),
        compiler_params=_cparams(("arbitrary",)),
        name="out_proj_head_lanes",
    )(y, g, w, x, gate)


def _scan_kernel(r_ref, wl_ref, k_ref, v_ref, al_ref, kk_ref, ka_ref, rk_ref,
                 lnw_ref, lnb_ref, o_ref,
                 s_ref, v_s, dec_s, kap_s, nka_s, km_s, yt_s):
    dh, tb, nl = r_ref.shape
    half = dh // 2

    @pl.when(pl.program_id(0) == 0)
    def _():
        s_ref[...] = jnp.zeros_like(s_ref)

    def seq(ref):
        return ref[...].reshape(dh, tb, nl)

    def put(ref, val):
        ref[...] = val.reshape(dh * tb, nl)

    put(v_s, v_ref[...])
    z = -wl_ref[...]
    softplus = jnp.maximum(z, 0.0) + jnp.log1p(jnp.exp(-jnp.abs(z)))
    put(dec_s, jnp.exp(-jnp.exp(-softplus - 0.5)))
    a = jax.nn.sigmoid(al_ref[...])
    k = k_ref[...]
    kk = k * kk_ref[...]
    nrm = jnp.sqrt(jnp.sum(kk * kk, axis=0, keepdims=True))
    kap = kk / jnp.maximum(nrm, 1e-12)
    put(kap_s, kap)
    put(nka_s, -(kap * a))
    put(km_s, k * (1 + (a - 1) * ka_ref[...]))

    def row(ref, t, c):
        return jnp.broadcast_to(ref[pl.ds(c * tb + t, 1), :], (half, nl))

    def r_row(t, c):
        return jnp.broadcast_to(r_ref[c, pl.ds(t, 1), :], (half, nl))

    def tile(hf, t):
        return pl.ds(hf * half * tb + t, half, stride=tb)

    zeros = jnp.zeros((half, nl), F32)

    def first_sa(hf):
        def body(g, acc):
            for j in range(SCAN_KU):
                c = g * SCAN_KU + j
                acc = acc + s_ref[hf, c] * row(kap_s, 0, c)
            return acc
        return lax.fori_loop(0, dh // SCAN_KU, body, zeros)

    def run_half(hf, t, t_next, sa):
        v_t = v_s[tile(hf, t), :]

        def body(g, carry):
            y, sa_next = carry
            for j in range(SCAN_KU):
                c = g * SCAN_KU + j
                s_new = (s_ref[hf, c] * row(dec_s, t, c) + sa * row(nka_s, t, c)
                         + v_t * row(km_s, t, c))
                s_ref[hf, c] = s_new
                y = y + s_new * r_row(t, c)
                sa_next = sa_next + s_new * row(kap_s, t_next, c)
            return y, sa_next

        y, sa_next = lax.fori_loop(0, dh // SCAN_KU, body, (zeros, zeros))
        yt_s[pl.ds(pl.multiple_of(t * dh + hf * half, half), half), :] = y
        return sa_next

    def step(t, sa):
        t_next = jnp.minimum(t + 1, tb - 1)
        return run_half(0, t, t_next, sa[0]), run_half(1, t, t_next, sa[1])

    lax.fori_loop(0, tb, step, (first_sa(0), first_sa(1)))

    for ch in range(dh):
        o_ref[ch] = yt_s[pl.ds(ch, tb, stride=dh), :]
    y = o_ref[...]
    mean = jnp.mean(y, axis=0, keepdims=True)
    d = y - mean
    var = jnp.mean(d * d, axis=0, keepdims=True)
    yn = d * lax.rsqrt(var + GN_EPS) * lnw_ref[...] + lnb_ref[...]
    bonus = jnp.sum(r_ref[...] * seq(km_s) * rk_ref[...], axis=0, keepdims=True)
    o_ref[...] = yn + bonus * v_ref[...]


def rwkv_scan(r, wl, k, v, al, k_k, k_a, r_k, ln_w, ln_b):
    dh, T, nl = r.shape
    tb = SCAN_TB
    seq = pl.BlockSpec((dh, tb, nl), lambda i: (0, i, 0))
    par = pl.BlockSpec((dh, 1, nl), lambda i: (0, 0, 0))
    return pl.pallas_call(
        _scan_kernel,
        out_shape=jax.ShapeDtypeStruct((dh, T, nl), F32),
        grid=(T // tb,),
        in_specs=[seq] * 5 + [par] * 5,
        out_specs=seq,
        scratch_shapes=([pltpu.VMEM((2, dh, dh // 2, nl), F32)]
                        + [pltpu.VMEM((dh * tb, nl), F32)] * 6),
        compiler_params=_cparams(("arbitrary",)),
        name="rwkv_scan",
    )(r, wl, k, v, al, k_k, k_a, r_k, ln_w, ln_b)


def _bias_kernel(g_ref, o_ref):
    nq, nw = o_ref.shape[1], o_ref.shape[2]
    ng = g_ref.shape[2]
    row = lax.broadcasted_iota(jnp.int32, (nq, ng), 0)
    x = jnp.broadcast_to(g_ref[0], (nq, ng))
    shift = 1
    while shift < nq:
        x = jnp.where((row & shift) != 0, pltpu.roll(x, shift, axis=1), x)
        shift *= 2
    qi = lax.broadcasted_iota(jnp.int32, (nq, nw), 0) // CHUNK
    kj = lax.broadcasted_iota(jnp.int32, (nq, nw), 1) // CHUNK
    band = jnp.logical_and(kj >= qi, kj <= qi + N_LEFT_CHUNKS)
    o_ref[0] = jnp.where(band, x[:, :nw], MASK_VALUE)


def band_bias(rel_bias):
    H = rel_bias.shape[0]
    ng = -(-(ATTN_W + ATTN_Q) // LANES) * LANES
    far = rel_bias[:, 2 * REL_CLIP:]
    near = rel_bias[:, 2 * REL_CLIP - 1:REL_CLIP + ATTN_PAD - ATTN_W:-1]
    n_far = ATTN_W - near.shape[1]
    g = jnp.concatenate(
        [jnp.broadcast_to(far, (H, n_far)), near, jnp.broadcast_to(far, (H, ng - ATTN_W))], axis=1)
    return pl.pallas_call(
        _bias_kernel,
        out_shape=jax.ShapeDtypeStruct((H, ATTN_Q, ATTN_W), F32),
        grid=(H,),
        in_specs=[pl.BlockSpec((1, 1, ng), lambda h: (h, 0, 0))],
        out_specs=pl.BlockSpec((1, ATTN_Q, ATTN_W), lambda h: (h, 0, 0)),
        compiler_params=_cparams(("arbitrary",)),
        name="band_bias",
    )(g.reshape(H, 1, ng))


def _attn_kernel(q_ref, k_ref, v_ref, b_ref, o_ref):
    T = q_ref.shape[1]
    n_steps = T // ATTN_Q
    n_masked = -(-ATTN_PAD // ATTN_Q)
    lane = lax.broadcasted_iota(jnp.int32, (ATTN_Q, LANES), 1)
    scale = HEAD_DIM ** -0.5
    head_scale = [jnp.where(lane < HEAD_DIM, scale, 0.0).astype(BF16),
                  jnp.where(lane >= HEAD_DIM, scale, 0.0).astype(BF16)]
    kpos = lax.broadcasted_iota(jnp.int32, (2 * ATTN_Q, ATTN_W), 1)

    def step(n, masked):
        start = pl.multiple_of(n * ATTN_Q, ATTN_Q)
        q2 = q_ref[0, pl.ds(start, ATTN_Q), :]
        kw = k_ref[0, pl.ds(start, ATTN_W), :]
        vw = v_ref[0, pl.ds(start, ATTN_W), :]
        qs = jnp.concatenate([q2 * head_scale[0], q2 * head_scale[1]], axis=0)
        s = lax.dot_general(qs, kw, (((1,), (1,)), ((), ())), preferred_element_type=F32)
        s = s + b_ref[...].reshape(2 * ATTN_Q, ATTN_W)
        if masked:
            s = jnp.where(kpos + start >= ATTN_PAD, s, MASK_VALUE)
        m = jnp.max(s, axis=-1, keepdims=True)
        p = jnp.exp(s - m)
        l = jnp.sum(p, axis=-1, keepdims=True)
        o = jnp.dot(p.astype(BF16), vw, preferred_element_type=F32) / l
        o_ref[0, pl.ds(start, ATTN_Q), :] = jnp.where(
            lane < HEAD_DIM, o[:ATTN_Q], o[ATTN_Q:]).astype(o_ref.dtype)

    def masked_body(n, carry):
        step(n, True)
        return carry

    def body(n, carry):
        step(n, False)
        return carry

    lax.fori_loop(0, n_masked, masked_body, 0, unroll=2)
    lax.fori_loop(n_masked, n_steps, body, 0, unroll=4)


def band_attention(q, kvp, bias):
    B, T, D = q.shape
    Tp = kvp.shape[1]
    n_pairs = D // LANES
    return pl.pallas_call(
        _attn_kernel,
        out_shape=jax.ShapeDtypeStruct((B, T, D), BF16),
        grid=(B, n_pairs),
        in_specs=[
            pl.BlockSpec((1, T, LANES), lambda b, p: (b, 0, p)),
            pl.BlockSpec((1, Tp, LANES), lambda b, p: (b, 0, p)),
            pl.BlockSpec((1, Tp, LANES), lambda b, p: (b, 0, p + n_pairs)),
            pl.BlockSpec((2, ATTN_Q, ATTN_W), lambda b, p: (p, 0, 0)),
        ],
        out_specs=pl.BlockSpec((1, T, LANES), lambda b, p: (b, 0, p)),
        compiler_params=_cparams(("arbitrary", "arbitrary")),
        name="band_attention",
    )(q, kvp, kvp, bias)


def _group_cols(w, H, group):
    lead = w.shape[:-1]
    w = w.reshape(lead + (H, HEAD_DIM // group, group))
    w = jnp.moveaxis(w, -3, -1)
    return w.reshape(lead + (H * HEAD_DIM,))


def _param_head_lanes(p, B, H, group):
    t = jnp.transpose(p.astype(F32).reshape(H, HEAD_DIM // group, group), (2, 1, 0))
    return jnp.tile(t.reshape(HEAD_DIM, H), (1, B)).reshape(HEAD_DIM, 1, B * H)


def kernel(x, c, w_ada, b_ada, g_mix, g_mlp, w_up, w_down, rwkv_mu, rwkv_w_r, rwkv_w_k, rwkv_w_v, rwkv_w_o, rwkv_w0, rwkv_w1, rwkv_w2, rwkv_a0, rwkv_a1, rwkv_a2, rwkv_g1, rwkv_g2, rwkv_k_k, rwkv_k_a, rwkv_r_k, rwkv_ln_w, rwkv_ln_b, attn_w_q, attn_w_o, attn_rel_bias, w_ada_kv, b_ada_kv, g_kv, w_k_shared, w_v_shared, g_final):
    B, T, D = x.shape
    H = D // HEAD_DIM
    M = B * T
    depth = w_ada.shape[0]
    n_a = rwkv_mu.shape[0]
    group = LANES // H
    assert T % ATTN_Q == 0 and T % SCAN_TB == 0 and B * H == LANES and group == B

    xf = x.reshape(M, D)
    c_pad = jnp.pad(c, ((0, 16 - B), (0, 0))).astype(BF16)
    mods = ada_mod(c_pad, w_ada, b_ada)[:, :B]
    mod_kv = ada_mod(c_pad, w_ada_kv[None], b_ada_kv[None])[0, :B]

    def split(m, n):
        return [p.reshape(B, 1, D) for p in jnp.split(m, n, axis=-1)]

    w_up_bf = w_up.astype(BF16)
    w_down_bf = w_down.astype(BF16)
    kvp = None
    for layer in range(depth):
        sh1, sc1, gt1, sh2, sc2, gt2 = split(mods[layer], 6)
        if layer == n_a:
            sh_kv, sc_kv = split(mod_kv, 2)
            h_kv = norm_mod(xf, g_kv, sh_kv, sc_kv, T)
            w_kv = jnp.concatenate([w_k_shared, w_v_shared], axis=1).astype(BF16)
            kv = matmul(h_kv, w_kv, out_dtype=BF16).reshape(B, T, 2 * D)
            kvp = jnp.pad(kv, ((0, 0), (ATTN_PAD, 0), (0, 0)))
        if layer < n_a:
            i = layer
            gc = functools.partial(_group_cols, H=H, group=group)
            hp = functools.partial(_param_head_lanes, B=B, H=H, group=group)
            xr, xw, xk, xv, xa, xg = premix(xf, g_mix[layer], sh1, sc1, rwkv_mu[i], T)
            rows = lambda a: a.reshape(B, T, D)
            r, g = proj_lora_head_lanes(
                rows(xr), gc(rwkv_w_r[i]).astype(BF16), rows(xg),
                *_pad_rank(rwkv_g1[i], gc(rwkv_g2[i])), jnp.zeros((D,), F32), "sigmoid", False)
            k, wl = proj_lora_head_lanes(
                rows(xk), gc(rwkv_w_k[i]).astype(BF16), rows(xw),
                *_pad_rank(rwkv_w1[i], gc(rwkv_w2[i])), gc(rwkv_w0[i]), "tanh", True)
            v, al = proj_lora_head_lanes(
                rows(xv), gc(rwkv_w_v[i]).astype(BF16), rows(xa),
                *_pad_rank(rwkv_a1[i], gc(rwkv_a2[i])), gc(rwkv_a0[i]), "none", True)
            y = rwkv_scan(r, wl, k, v, al,
                          hp(rwkv_k_k[i]), hp(rwkv_k_a[i]), hp(rwkv_r_k[i].reshape(D)),
                          hp(rwkv_ln_w[i]), hp(rwkv_ln_b[i]))
            w_o = gc(rwkv_w_o[i].T).T
            xf = out_proj_head_lanes(y, g, w_o.astype(BF16), rows(xf), gt1).reshape(M, D)
        else:
            i = layer - n_a
            h = norm_mod(xf, g_mix[layer], sh1, sc1, T)
            q = matmul(h, attn_w_q[i].astype(BF16), out_dtype=BF16, **SQUARE_TILES).reshape(B, T, D)
            bias = band_bias(attn_rel_bias[i].astype(F32))
            yg = band_attention(q, kvp, bias).reshape(M, D)
            xf = matmul(yg, attn_w_o[i].astype(BF16), epilogue="resid", resid=xf, gate=gt1,
                        rows_per_batch=T, **SQUARE_TILES)
        h = norm_mod(xf, g_mlp[layer], sh2, sc2, T)
        hid = matmul(h, w_up_bf, layer=layer, epilogue="relu2", out_dtype=BF16)
        xf = matmul(hid, w_down_bf, layer=layer, epilogue="resid", resid=xf, gate=gt2,
                    rows_per_batch=T)
    return final_norm(xf, g_final).reshape(B, T, D)
```

```python
import functools

import jax
import jax.numpy as jnp
from jax import lax
from jax.experimental import pallas as pl
from jax.experimental.pallas import tpu as pltpu

F32 = jnp.float32
BF16 = jnp.bfloat16

CHUNK = 64
N_LEFT_CHUNKS = 8
HEAD_DIM = 64
REL_CLIP = 256
RMS_EPS = 1e-6
GN_EPS = 64e-5
MASK_VALUE = -1e30

LANES = 128
ATTN_Q = 2 * CHUNK
ATTN_PAD = N_LEFT_CHUNKS * CHUNK
ATTN_W = ATTN_PAD + ATTN_Q
SCAN_TB = 64
SCAN_KU = 64
VMEM_LIMIT = 48 * 1024 * 1024
SQUARE_TILES = dict(tm=512, tn=2048)


def _cparams(sem):
    return pltpu.CompilerParams(dimension_semantics=sem, vmem_limit_bytes=VMEM_LIMIT)


def _ada_kernel(c_ref, w_ref, b_ref, o_ref):
    w = w_ref[0].astype(BF16)
    o_ref[0] = jnp.dot(c_ref[...], w, preferred_element_type=F32) + b_ref[0]


def ada_mod(c_pad, w, b, tn=1024):
    L, D, N = w.shape
    R = c_pad.shape[0]
    return pl.pallas_call(
        _ada_kernel,
        out_shape=jax.ShapeDtypeStruct((L, R, N), F32),
        grid=(L, N // tn),
        in_specs=[
            pl.BlockSpec((R, D), lambda l, j: (0, 0)),
            pl.BlockSpec((1, D, tn), lambda l, j: (l, 0, j)),
            pl.BlockSpec((1, 1, tn), lambda l, j: (l, 0, j)),
        ],
        out_specs=pl.BlockSpec((1, R, tn), lambda l, j: (l, 0, j)),
        compiler_params=_cparams(("arbitrary", "arbitrary")),
        name="ada_mod",
    )(c_pad, w, b.reshape(L, 1, N))


def _rms(x, g):
    ms = jnp.mean(x * x, axis=-1, keepdims=True)
    return x * lax.rsqrt(ms + RMS_EPS) * g


def _norm_mod_kernel(x_ref, g_ref, sh_ref, sc_ref, o_ref):
    y = _rms(x_ref[...], g_ref[...])
    o_ref[...] = (y * (1 + sc_ref[0]) + sh_ref[0]).astype(o_ref.dtype)


def norm_mod(x, g, sh, sc, rows_per_batch, tm=512):
    M, D = x.shape
    per = rows_per_batch // tm
    return pl.pallas_call(
        _norm_mod_kernel,
        out_shape=jax.ShapeDtypeStruct((M, D), BF16),
        grid=(M // tm,),
        in_specs=[
            pl.BlockSpec((tm, D), lambda i: (i, 0)),
            pl.BlockSpec((1, D), lambda i: (0, 0)),
            pl.BlockSpec((1, 1, D), lambda i: (i // per, 0, 0)),
            pl.BlockSpec((1, 1, D), lambda i: (i // per, 0, 0)),
        ],
        out_specs=pl.BlockSpec((tm, D), lambda i: (i, 0)),
        compiler_params=_cparams(("arbitrary",)),
        name="norm_mod",
    )(x, g.reshape(1, D), sh, sc)


def _final_norm_kernel(x_ref, g_ref, o_ref):
    o_ref[...] = _rms(x_ref[...], g_ref[...])


def final_norm(x, g, tm=512):
    M, D = x.shape
    return pl.pallas_call(
        _final_norm_kernel,
        out_shape=jax.ShapeDtypeStruct((M, D), F32),
        grid=(M // tm,),
        in_specs=[
            pl.BlockSpec((tm, D), lambda i: (i, 0)),
            pl.BlockSpec((1, D), lambda i: (0, 0)),
        ],
        out_specs=pl.BlockSpec((tm, D), lambda i: (i, 0)),
        compiler_params=_cparams(("arbitrary",)),
        name="final_norm",
    )(x, g.reshape(1, D))


def _premix_kernel(per, x_ref, xp_ref, g_ref, sh_ref, sc_ref, mu_ref, *o_refs):
    g = g_ref[...]
    sc = 1 + sc_ref[0]
    sh = sh_ref[0]
    h = _rms(x_ref[...], g) * sc + sh
    hp = (_rms(xp_ref[...], g) * sc + sh)[7:8, :]
    first_tile = (pl.program_id(0) % per) == 0
    hp = jnp.where(first_tile, 0.0, hp)
    row = lax.broadcasted_iota(jnp.int32, h.shape, 0)
    hs = jnp.where(row == 0, hp, pltpu.roll(h, 1, axis=0))
    xx = hs - h
    for j, o_ref in enumerate(o_refs):
        o_ref[...] = (h + xx * mu_ref[j:j + 1, :]).astype(o_ref.dtype)


def premix(x, g, sh, sc, mu, rows_per_batch, tm=256):
    M, D = x.shape
    per = rows_per_batch // tm
    sub = 8
    n_mix = mu.shape[0]
    return pl.pallas_call(
        functools.partial(_premix_kernel, per),
        out_shape=[jax.ShapeDtypeStruct((M, D), BF16)] * n_mix,
        grid=(M // tm,),
        in_specs=[
            pl.BlockSpec((tm, D), lambda i: (i, 0)),
            pl.BlockSpec((sub, D), lambda i: (jnp.maximum(i * (tm // sub) - 1, 0), 0)),
            pl.BlockSpec((1, D), lambda i: (0, 0)),
            pl.BlockSpec((1, 1, D), lambda i: (i // per, 0, 0)),
            pl.BlockSpec((1, 1, D), lambda i: (i // per, 0, 0)),
            pl.BlockSpec((n_mix, D), lambda i: (0, 0)),
        ],
        out_specs=[pl.BlockSpec((tm, D), lambda i: (i, 0))] * n_mix,
        compiler_params=_cparams(("arbitrary",)),
        name="premix",
    )(x, x, g.reshape(1, D), sh, sc, mu)


def _mm_kernel(epilogue, nk, a_ref, w_ref, *rest):
    if epilogue == "resid":
        x_ref, gt_ref, o_ref, acc_ref = rest
    else:
        o_ref, acc_ref = rest
    k = pl.program_id(2)
    w = w_ref[0] if len(w_ref.shape) == 3 else w_ref[...]
    part = jnp.dot(a_ref[...], w, preferred_element_type=F32)

    def finish(acc):
        if epilogue == "resid":
            o_ref[...] = x_ref[...] + gt_ref[0] * acc
        elif epilogue == "relu2":
            o_ref[...] = jnp.square(jnp.maximum(acc, 0.0)).astype(o_ref.dtype)
        else:
            o_ref[...] = acc.astype(o_ref.dtype)

    if nk == 1:
        finish(part)
    else:
        @pl.when(k == 0)
        def _():
            acc_ref[...] = part

        @pl.when(jnp.logical_and(k > 0, k < nk - 1))
        def _():
            acc_ref[...] += part

        @pl.when(k == nk - 1)
        def _():
            finish(acc_ref[...] + part)


def matmul(a, w, *, layer=None, epilogue="none", out_dtype=F32, resid=None, gate=None,
           rows_per_batch=None, tm=1024, tn=1024, tk=2048):
    M, K = a.shape
    N = w.shape[-1]
    tk = min(tk, K)
    tn = min(tn, N)
    nk = K // tk
    if layer is None:
        w_spec = pl.BlockSpec((tk, tn), lambda i, j, k: (k, j))
    else:
        w_spec = pl.BlockSpec((1, tk, tn), lambda i, j, k: (layer, k, j))
    in_specs = [pl.BlockSpec((tm, tk), lambda i, j, k: (i, k)), w_spec]
    args = [a, w]
    if epilogue == "resid":
        per = rows_per_batch // tm
        in_specs += [
            pl.BlockSpec((tm, tn), lambda i, j, k: (i, j)),
            pl.BlockSpec((1, 1, tn), lambda i, j, k: (i // per, 0, j)),
        ]
        args += [resid, gate]
    acc_shape = (tm, tn) if nk > 1 else (8, LANES)
    return pl.pallas_call(
        functools.partial(_mm_kernel, epilogue, nk),
        out_shape=jax.ShapeDtypeStruct((M, N), out_dtype),
        grid=(M // tm, N // tn, nk),
        in_specs=in_specs,
        out_specs=pl.BlockSpec((tm, tn), lambda i, j, k: (i, j)),
        scratch_shapes=[pltpu.VMEM(acc_shape, F32)],
        compiler_params=_cparams(("arbitrary", "arbitrary", "arbitrary")),
        name="mm_" + epilogue,
    )(*args)


def _pad_rank(w1, w2):
    r = w1.shape[1]
    rp = -(-r // LANES) * LANES
    if rp != r:
        w1 = jnp.pad(w1, ((0, 0), (0, rp - r)))
        w2 = jnp.pad(w2, ((0, rp - r), (0, 0)))
    return w1.astype(BF16), w2.astype(BF16)


def _transpose_lane_groups(xs):
    nl = xs[0].shape[-1]
    n = len(xs)
    width = nl // n
    grp = lax.broadcasted_iota(jnp.int32, xs[0].shape, 1) // width
    ys = []
    for c in range(n):
        acc = None
        for b in range(n):
            shift = ((b - c) * width) % nl
            piece = pltpu.roll(xs[b], shift, axis=1) if shift else xs[b]
            acc = piece if acc is None else jnp.where(grp == b, piece, acc)
        ys.append(acc)
    return ys


def _rows_to_head_lanes(x, nb, o_ref):
    tq = x.shape[0] // nb
    ng = x.shape[1] // LANES
    for g in range(ng):
        xs = [x[b * tq:(b + 1) * tq, g * LANES:(g + 1) * LANES] for b in range(nb)]
        for c, y in enumerate(_transpose_lane_groups(xs)):
            o_ref[c * ng + g] = y


def _head_lanes_to_rows(y_ref, nb):
    ng = y_ref.shape[0] // nb
    cols = [_transpose_lane_groups([y_ref[c * ng + g] for c in range(nb)]) for g in range(ng)]
    return [jnp.concatenate([cols[g][b] for g in range(ng)], axis=1) for b in range(nb)]


def _proj_lora_kernel(act, lora_hl, a_ref, w_ref, al_ref, w1_ref, w2_ref, b_ref, o_ref, ol_ref):
    nb, tq, kdim = a_ref.shape
    acc = jnp.dot(a_ref[...].reshape(nb * tq, kdim), w_ref[...], preferred_element_type=F32)
    _rows_to_head_lanes(acc, nb, o_ref)
    t = jnp.dot(al_ref[...].reshape(nb * tq, kdim), w1_ref[...], preferred_element_type=F32)
    if act == "tanh":
        t = jnp.tanh(t)
    elif act == "sigmoid":
        t = jax.nn.sigmoid(t)
    o = jnp.dot(t.astype(BF16), w2_ref[...], preferred_element_type=F32) + b_ref[...]
    if lora_hl:
        _rows_to_head_lanes(o, nb, ol_ref)
    else:
        ol_ref[...] = o.reshape(nb, tq, o.shape[1])


def proj_lora_head_lanes(a, w, a_lora, w1, w2, bias, act, lora_hl, tq=128):
    B, T, K = a.shape
    D = w.shape[1]
    R = w1.shape[1]
    hl = jax.ShapeDtypeStruct((HEAD_DIM, T, LANES), F32)
    hl_spec = pl.BlockSpec((HEAD_DIM, tq, LANES), lambda i: (0, i, 0))
    row_spec = pl.BlockSpec((B, tq, D), lambda i: (0, i, 0))
    return pl.pallas_call(
        functools.partial(_proj_lora_kernel, act, lora_hl),
        out_shape=[hl, hl if lora_hl else jax.ShapeDtypeStruct((B, T, D), F32)],
        grid=(T // tq,),
        in_specs=[
            pl.BlockSpec((B, tq, K), lambda i: (0, i, 0)),
            pl.BlockSpec((K, D), lambda i: (0, 0)),
            pl.BlockSpec((B, tq, K), lambda i: (0, i, 0)),
            pl.BlockSpec((K, R), lambda i: (0, 0)),
            pl.BlockSpec((R, D), lambda i: (0, 0)),
            pl.BlockSpec((1, D), lambda i: (0, 0)),
        ],
        out_specs=[hl_spec, hl_spec if lora_hl else row_spec],
        compiler_params=_cparams(("arbitrary",)),
        name="proj_lora_" + act,
    )(a, w, a_lora, w1, w2, bias.reshape(1, D))


def _out_proj_hl_kernel(y_ref, g_ref, w_ref, x_ref, gt_ref, o_ref):
    nb, tq, d = g_ref.shape
    ys = _head_lanes_to_rows(y_ref, nb)
    a = jnp.concatenate([(ys[b] * g_ref[b]).astype(BF16) for b in range(nb)], axis=0)
    acc = jnp.dot(a, w_ref[...], preferred_element_type=F32)
    o_ref[...] = x_ref[...] + gt_ref[...] * acc.reshape(nb, tq, acc.shape[1])


def out_proj_head_lanes(y, g, w, x, gate, tq=64):
    B, T, D = g.shape
    N = w.shape[1]
    return pl.pallas_call(
        _out_proj_hl_kernel,
        out_shape=jax.ShapeDtypeStruct((B, T, N), F32),
        grid=(T // tq,),
        in_specs=[
            pl.BlockSpec((HEAD_DIM, tq, LANES), lambda i: (0, i, 0)),
            pl.BlockSpec((B, tq, D), lambda i: (0, i, 0)),
            pl.BlockSpec((D, N), lambda i: (0, 0)),
            pl.BlockSpec((B, tq, N), lambda i: (0, i, 0)),
            pl.BlockSpec((B, 1, N), lambda i: (0, 0, 0)),
        ],
        out_specs=pl.BlockSpec((B, tq, N), lambda i: (0, i, 0)),
        compiler_params=_cparams(("arbitrary",)),
        name="out_proj_head_lanes",
    )(y, g, w, x, gate)


def _scan_kernel(r_ref, wl_ref, k_ref, v_ref, al_ref, kk_ref, ka_ref, rk_ref,
                 lnw_ref, lnb_ref, o_ref,
                 s_ref, v_s, dec_s, kap_s, nka_s, km_s, yt_s):
    dh, tb, nl = r_ref.shape
    half = dh // 2

    @pl.when(pl.program_id(0) == 0)
    def _():
        s_ref[...] = jnp.zeros_like(s_ref)

    def seq(ref):
        return ref[...].reshape(dh, tb, nl)

    def put(ref, val):
        ref[...] = val.reshape(dh * tb, nl)

    put(v_s, v_ref[...])
    z = -wl_ref[...]
    softplus = jnp.maximum(z, 0.0) + jnp.log1p(jnp.exp(-jnp.abs(z)))
    put(dec_s, jnp.exp(-jnp.exp(-softplus - 0.5)))
    a = jax.nn.sigmoid(al_ref[...])
    k = k_ref[...]
    kk = k * kk_ref[...]
    nrm = jnp.sqrt(jnp.sum(kk * kk, axis=0, keepdims=True))
    kap = kk / jnp.maximum(nrm, 1e-12)
    put(kap_s, kap)
    put(nka_s, -(kap * a))
    put(km_s, k * (1 + (a - 1) * ka_ref[...]))

    def row(ref, t, c):
        return jnp.broadcast_to(ref[pl.ds(c * tb + t, 1), :], (half, nl))

    def r_row(t, c):
        return jnp.broadcast_to(r_ref[c, pl.ds(t, 1), :], (half, nl))

    def tile(hf, t):
        return pl.ds(hf * half * tb + t, half, stride=tb)

    zeros = jnp.zeros((half, nl), F32)

    def first_sa(hf):
        def body(g, acc):
            for j in range(SCAN_KU):
                c = g * SCAN_KU + j
                acc = acc + s_ref[hf, c] * row(kap_s, 0, c)
            return acc
        return lax.fori_loop(0, dh // SCAN_KU, body, zeros)

    def run_half(hf, t, t_next, sa):
        v_t = v_s[tile(hf, t), :]

        def body(g, carry):
            y, sa_next = carry
            for j in range(SCAN_KU):
                c = g * SCAN_KU + j
                s_new = (s_ref[hf, c] * row(dec_s, t, c) + sa * row(nka_s, t, c)
                         + v_t * row(km_s, t, c))
                s_ref[hf, c] = s_new
                y = y + s_new * r_row(t, c)
                sa_next = sa_next + s_new * row(kap_s, t_next, c)
            return y, sa_next

        y, sa_next = lax.fori_loop(0, dh // SCAN_KU, body, (zeros, zeros))
        yt_s[pl.ds(pl.multiple_of(t * dh + hf * half, half), half), :] = y
        return sa_next

    def step(t, sa):
        t_next = jnp.minimum(t + 1, tb - 1)
        return run_half(0, t, t_next, sa[0]), run_half(1, t, t_next, sa[1])

    lax.fori_loop(0, tb, step, (first_sa(0), first_sa(1)))

    for ch in range(dh):
        o_ref[ch] = yt_s[pl.ds(ch, tb, stride=dh), :]
    y = o_ref[...]
    mean = jnp.mean(y, axis=0, keepdims=True)
    d = y - mean
    var = jnp.mean(d * d, axis=0, keepdims=True)
    yn = d * lax.rsqrt(var + GN_EPS) * lnw_ref[...] + lnb_ref[...]
    bonus = jnp.sum(r_ref[...] * seq(km_s) * rk_ref[...], axis=0, keepdims=True)
    o_ref[...] = yn + bonus * v_ref[...]


def rwkv_scan(r, wl, k, v, al, k_k, k_a, r_k, ln_w, ln_b):
    dh, T, nl = r.shape
    tb = SCAN_TB
    seq = pl.BlockSpec((dh, tb, nl), lambda i: (0, i, 0))
    par = pl.BlockSpec((dh, 1, nl), lambda i: (0, 0, 0))
    return pl.pallas_call(
        _scan_kernel,
        out_shape=jax.ShapeDtypeStruct((dh, T, nl), F32),
        grid=(T // tb,),
        in_specs=[seq] * 5 + [par] * 5,
        out_specs=seq,
        scratch_shapes=([pltpu.VMEM((2, dh, dh // 2, nl), F32)]
                        + [pltpu.VMEM((dh * tb, nl), F32)] * 6),
        compiler_params=_cparams(("arbitrary",)),
        name="rwkv_scan",
    )(r, wl, k, v, al, k_k, k_a, r_k, ln_w, ln_b)


def _bias_kernel(g_ref, o_ref):
    nq, nw = o_ref.shape[1], o_ref.shape[2]
    ng = g_ref.shape[2]
    row = lax.broadcasted_iota(jnp.int32, (nq, ng), 0)
    x = jnp.broadcast_to(g_ref[0], (nq, ng))
    shift = 1
    while shift < nq:
        x = jnp.where((row & shift) != 0, pltpu.roll(x, shift, axis=1), x)
        shift *= 2
    qi = lax.broadcasted_iota(jnp.int32, (nq, nw), 0) // CHUNK
    kj = lax.broadcasted_iota(jnp.int32, (nq, nw), 1) // CHUNK
    band = jnp.logical_and(kj >= qi, kj <= qi + N_LEFT_CHUNKS)
    o_ref[0] = jnp.where(band, x[:, :nw], MASK_VALUE)


def band_bias(rel_bias):
    H = rel_bias.shape[0]
    ng = -(-(ATTN_W + ATTN_Q) // LANES) * LANES
    far = rel_bias[:, 2 * REL_CLIP:]
    near = rel_bias[:, 2 * REL_CLIP - 1:REL_CLIP + ATTN_PAD - ATTN_W:-1]
    n_far = ATTN_W - near.shape[1]
    g = jnp.concatenate(
        [jnp.broadcast_to(far, (H, n_far)), near, jnp.broadcast_to(far, (H, ng - ATTN_W))], axis=1)
    return pl.pallas_call(
        _bias_kernel,
        out_shape=jax.ShapeDtypeStruct((H, ATTN_Q, ATTN_W), F32),
        grid=(H,),
        in_specs=[pl.BlockSpec((1, 1, ng), lambda h: (h, 0, 0))],
        out_specs=pl.BlockSpec((1, ATTN_Q, ATTN_W), lambda h: (h, 0, 0)),
        compiler_params=_cparams(("arbitrary",)),
        name="band_bias",
    )(g.reshape(H, 1, ng))


def _attn_kernel(q_ref, k_ref, v_ref, b_ref, o_ref):
    T = q_ref.shape[1]
    n_steps = T // ATTN_Q
    n_masked = -(-ATTN_PAD // ATTN_Q)
    lane = lax.broadcasted_iota(jnp.int32, (ATTN_Q, LANES), 1)
    scale = HEAD_DIM ** -0.5
    head_scale = [jnp.where(lane < HEAD_DIM, scale, 0.0).astype(BF16),
                  jnp.where(lane >= HEAD_DIM, scale, 0.0).astype(BF16)]
    kpos = lax.broadcasted_iota(jnp.int32, (2 * ATTN_Q, ATTN_W), 1)

    def step(n, masked):
        start = pl.multiple_of(n * ATTN_Q, ATTN_Q)
        q2 = q_ref[0, pl.ds(start, ATTN_Q), :]
        kw = k_ref[0, pl.ds(start, ATTN_W), :]
        vw = v_ref[0, pl.ds(start, ATTN_W), :]
        qs = jnp.concatenate([q2 * head_scale[0], q2 * head_scale[1]], axis=0)
        s = lax.dot_general(qs, kw, (((1,), (1,)), ((), ())), preferred_element_type=F32)
        s = s + b_ref[...].reshape(2 * ATTN_Q, ATTN_W)
        if masked:
            s = jnp.where(kpos + start >= ATTN_PAD, s, MASK_VALUE)
        m = jnp.max(s, axis=-1, keepdims=True)
        p = jnp.exp(s - m)
        l = jnp.sum(p, axis=-1, keepdims=True)
        o = jnp.dot(p.astype(BF16), vw, preferred_element_type=F32) / l
        o_ref[0, pl.ds(start, ATTN_Q), :] = jnp.where(
            lane < HEAD_DIM, o[:ATTN_Q], o[ATTN_Q:]).astype(o_ref.dtype)

    def masked_body(n, carry):
        step(n, True)
        return carry

    def body(n, carry):
        step(n, False)
        return carry

    lax.fori_loop(0, n_masked, masked_body, 0, unroll=4)
    lax.fori_loop(n_masked, n_steps, body, 0, unroll=4)


def band_attention(q, kvp, bias):
    B, T, D = q.shape
    Tp = kvp.shape[1]
    n_pairs = D // LANES
    return pl.pallas_call(
        _attn_kernel,
        out_shape=jax.ShapeDtypeStruct((B, T, D), BF16),
        grid=(B, n_pairs),
        in_specs=[
            pl.BlockSpec((1, T, LANES), lambda b, p: (b, 0, p)),
            pl.BlockSpec((1, Tp, LANES), lambda b, p: (b, 0, p)),
            pl.BlockSpec((1, Tp, LANES), lambda b, p: (b, 0, p + n_pairs)),
            pl.BlockSpec((2, ATTN_Q, ATTN_W), lambda b, p: (p, 0, 0)),
        ],
        out_specs=pl.BlockSpec((1, T, LANES), lambda b, p: (b, 0, p)),
        compiler_params=_cparams(("arbitrary", "arbitrary")),
        name="band_attention",
    )(q, kvp, kvp, bias)


def _group_cols(w, H, group):
    lead = w.shape[:-1]
    w = w.reshape(lead + (H, HEAD_DIM // group, group))
    w = jnp.moveaxis(w, -3, -1)
    return w.reshape(lead + (H * HEAD_DIM,))


def _param_head_lanes(p, B, H, group):
    t = jnp.transpose(p.astype(F32).reshape(H, HEAD_DIM // group, group), (2, 1, 0))
    return jnp.tile(t.reshape(HEAD_DIM, H), (1, B)).reshape(HEAD_DIM, 1, B * H)


def kernel(x, c, w_ada, b_ada, g_mix, g_mlp, w_up, w_down, rwkv_mu, rwkv_w_r, rwkv_w_k, rwkv_w_v, rwkv_w_o, rwkv_w0, rwkv_w1, rwkv_w2, rwkv_a0, rwkv_a1, rwkv_a2, rwkv_g1, rwkv_g2, rwkv_k_k, rwkv_k_a, rwkv_r_k, rwkv_ln_w, rwkv_ln_b, attn_w_q, attn_w_o, attn_rel_bias, w_ada_kv, b_ada_kv, g_kv, w_k_shared, w_v_shared, g_final):
    B, T, D = x.shape
    H = D // HEAD_DIM
    M = B * T
    depth = w_ada.shape[0]
    n_a = rwkv_mu.shape[0]
    group = LANES // H
    assert T % ATTN_Q == 0 and T % SCAN_TB == 0 and B * H == LANES and group == B

    xf = x.reshape(M, D)
    c_pad = jnp.pad(c, ((0, 16 - B), (0, 0))).astype(BF16)
    mods = ada_mod(c_pad, w_ada, b_ada)[:, :B]
    mod_kv = ada_mod(c_pad, w_ada_kv[None], b_ada_kv[None])[0, :B]

    def split(m, n):
        return [p.reshape(B, 1, D) for p in jnp.split(m, n, axis=-1)]

    w_up_bf = w_up.astype(BF16)
    w_down_bf = w_down.astype(BF16)
    kvp = None
    for layer in range(depth):
        sh1, sc1, gt1, sh2, sc2, gt2 = split(mods[layer], 6)
        if layer == n_a:
            sh_kv, sc_kv = split(mod_kv, 2)
            h_kv = norm_mod(xf, g_kv, sh_kv, sc_kv, T)
            w_kv = jnp.concatenate([w_k_shared, w_v_shared], axis=1).astype(BF16)
            kv = matmul(h_kv, w_kv, out_dtype=BF16).reshape(B, T, 2 * D)
            kvp = jnp.pad(kv, ((0, 0), (ATTN_PAD, 0), (0, 0)))
        if layer < n_a:
            i = layer
            gc = functools.partial(_group_cols, H=H, group=group)
            hp = functools.partial(_param_head_lanes, B=B, H=H, group=group)
            xr, xw, xk, xv, xa, xg = premix(xf, g_mix[layer], sh1, sc1, rwkv_mu[i], T)
            rows = lambda a: a.reshape(B, T, D)
            r, g = proj_lora_head_lanes(
                rows(xr), gc(rwkv_w_r[i]).astype(BF16), rows(xg),
                *_pad_rank(rwkv_g1[i], gc(rwkv_g2[i])), jnp.zeros((D,), F32), "sigmoid", False)
            k, wl = proj_lora_head_lanes(
                rows(xk), gc(rwkv_w_k[i]).astype(BF16), rows(xw),
                *_pad_rank(rwkv_w1[i], gc(rwkv_w2[i])), gc(rwkv_w0[i]), "tanh", True)
            v, al = proj_lora_head_lanes(
                rows(xv), gc(rwkv_w_v[i]).astype(BF16), rows(xa),
                *_pad_rank(rwkv_a1[i], gc(rwkv_a2[i])), gc(rwkv_a0[i]), "none", True)
            y = rwkv_scan(r, wl, k, v, al,
                          hp(rwkv_k_k[i]), hp(rwkv_k_a[i]), hp(rwkv_r_k[i].reshape(D)),
                          hp(rwkv_ln_w[i]), hp(rwkv_ln_b[i]))
            w_o = gc(rwkv_w_o[i].T).T
            xf = out_proj_head_lanes(y, g, w_o.astype(BF16), rows(xf), gt1).reshape(M, D)
        else:
            i = layer - n_a
            h = norm_mod(xf, g_mix[layer], sh1, sc1, T)
            q = matmul(h, attn_w_q[i].astype(BF16), out_dtype=BF16, **SQUARE_TILES).reshape(B, T, D)
            bias = band_bias(attn_rel_bias[i].astype(F32))
            yg = band_attention(q, kvp, bias).reshape(M, D)
            xf = matmul(yg, attn_w_o[i].astype(BF16), epilogue="resid", resid=xf, gate=gt1,
                        rows_per_batch=T, **SQUARE_TILES)
        h = norm_mod(xf, g_mlp[layer], sh2, sc2, T)
        hid = matmul(h, w_up_bf, layer=layer, epilogue="relu2", out_dtype=BF16, tn=2048)
        xf = matmul(hid, w_down_bf, layer=layer, epilogue="resid", resid=xf, gate=gt2,
                    rows_per_batch=T)
    return final_norm(xf, g_final).reshape(B, T, D)
```

```python
import functools

import jax
import jax.numpy as jnp
from jax import lax
from jax.experimental import pallas as pl
from jax.experimental.pallas import tpu as pltpu

F32 = jnp.float32
BF16 = jnp.bfloat16

CHUNK = 64
N_LEFT_CHUNKS = 8
HEAD_DIM = 64
REL_CLIP = 256
RMS_EPS = 1e-6
GN_EPS = 64e-5
MASK_VALUE = -1e30

LANES = 128
ATTN_Q = 2 * CHUNK
ATTN_PAD = N_LEFT_CHUNKS * CHUNK
ATTN_W = ATTN_PAD + ATTN_Q
SCAN_TB = 64
SCAN_KU = 64
VMEM_LIMIT = 48 * 1024 * 1024
SQUARE_TILES = dict(tm=512, tn=2048)


def _cparams(sem):
    return pltpu.CompilerParams(dimension_semantics=sem, vmem_limit_bytes=VMEM_LIMIT)


def _ada_kernel(c_ref, w_ref, b_ref, o_ref):
    w = w_ref[0].astype(BF16)
    o_ref[0] = jnp.dot(c_ref[...], w, preferred_element_type=F32) + b_ref[0]


def ada_mod(c_pad, w, b, tn=1024):
    L, D, N = w.shape
    R = c_pad.shape[0]
    return pl.pallas_call(
        _ada_kernel,
        out_shape=jax.ShapeDtypeStruct((L, R, N), F32),
        grid=(L, N // tn),
        in_specs=[
            pl.BlockSpec((R, D), lambda l, j: (0, 0)),
            pl.BlockSpec((1, D, tn), lambda l, j: (l, 0, j)),
            pl.BlockSpec((1, 1, tn), lambda l, j: (l, 0, j)),
        ],
        out_specs=pl.BlockSpec((1, R, tn), lambda l, j: (l, 0, j)),
        compiler_params=_cparams(("arbitrary", "arbitrary")),
        name="ada_mod",
    )(c_pad, w, b.reshape(L, 1, N))


def _rms(x, g):
    ms = jnp.mean(x * x, axis=-1, keepdims=True)
    return x * lax.rsqrt(ms + RMS_EPS) * g


def _norm_mod_kernel(x_ref, g_ref, sh_ref, sc_ref, o_ref):
    y = _rms(x_ref[...], g_ref[...])
    o_ref[...] = (y * (1 + sc_ref[0]) + sh_ref[0]).astype(o_ref.dtype)


def norm_mod(x, g, sh, sc, rows_per_batch, tm=1024):
    M, D = x.shape
    per = rows_per_batch // tm
    return pl.pallas_call(
        _norm_mod_kernel,
        out_shape=jax.ShapeDtypeStruct((M, D), BF16),
        grid=(M // tm,),
        in_specs=[
            pl.BlockSpec((tm, D), lambda i: (i, 0)),
            pl.BlockSpec((1, D), lambda i: (0, 0)),
            pl.BlockSpec((1, 1, D), lambda i: (i // per, 0, 0)),
            pl.BlockSpec((1, 1, D), lambda i: (i // per, 0, 0)),
        ],
        out_specs=pl.BlockSpec((tm, D), lambda i: (i, 0)),
        compiler_params=_cparams(("arbitrary",)),
        name="norm_mod",
    )(x, g.reshape(1, D), sh, sc)


def _final_norm_kernel(x_ref, g_ref, o_ref):
    o_ref[...] = _rms(x_ref[...], g_ref[...])


def final_norm(x, g, tm=1024):
    M, D = x.shape
    return pl.pallas_call(
        _final_norm_kernel,
        out_shape=jax.ShapeDtypeStruct((M, D), F32),
        grid=(M // tm,),
        in_specs=[
            pl.BlockSpec((tm, D), lambda i: (i, 0)),
            pl.BlockSpec((1, D), lambda i: (0, 0)),
        ],
        out_specs=pl.BlockSpec((tm, D), lambda i: (i, 0)),
        compiler_params=_cparams(("arbitrary",)),
        name="final_norm",
    )(x, g.reshape(1, D))


def _premix_kernel(per, x_ref, xp_ref, g_ref, sh_ref, sc_ref, mu_ref, *o_refs):
    g = g_ref[...]
    sc = 1 + sc_ref[0]
    sh = sh_ref[0]
    h = _rms(x_ref[...], g) * sc + sh
    hp = (_rms(xp_ref[...], g) * sc + sh)[7:8, :]
    first_tile = (pl.program_id(0) % per) == 0
    hp = jnp.where(first_tile, 0.0, hp)
    row = lax.broadcasted_iota(jnp.int32, h.shape, 0)
    hs = jnp.where(row == 0, hp, pltpu.roll(h, 1, axis=0))
    xx = hs - h
    for j, o_ref in enumerate(o_refs):
        o_ref[...] = (h + xx * mu_ref[j:j + 1, :]).astype(o_ref.dtype)


def premix(x, g, sh, sc, mu, rows_per_batch, tm=512):
    M, D = x.shape
    per = rows_per_batch // tm
    sub = 8
    n_mix = mu.shape[0]
    return pl.pallas_call(
        functools.partial(_premix_kernel, per),
        out_shape=[jax.ShapeDtypeStruct((M, D), BF16)] * n_mix,
        grid=(M // tm,),
        in_specs=[
            pl.BlockSpec((tm, D), lambda i: (i, 0)),
            pl.BlockSpec((sub, D), lambda i: (jnp.maximum(i * (tm // sub) - 1, 0), 0)),
            pl.BlockSpec((1, D), lambda i: (0, 0)),
            pl.BlockSpec((1, 1, D), lambda i: (i // per, 0, 0)),
            pl.BlockSpec((1, 1, D), lambda i: (i // per, 0, 0)),
            pl.BlockSpec((n_mix, D), lambda i: (0, 0)),
        ],
        out_specs=[pl.BlockSpec((tm, D), lambda i: (i, 0))] * n_mix,
        compiler_params=_cparams(("arbitrary",)),
        name="premix",
    )(x, x, g.reshape(1, D), sh, sc, mu)


def _mm_kernel(epilogue, nk, a_ref, w_ref, *rest):
    if epilogue == "resid":
        x_ref, gt_ref, o_ref, acc_ref = rest
    else:
        o_ref, acc_ref = rest
    k = pl.program_id(2)
    w = w_ref[0] if len(w_ref.shape) == 3 else w_ref[...]
    part = jnp.dot(a_ref[...], w, preferred_element_type=F32)

    def finish(acc):
        if epilogue == "resid":
            o_ref[...] = x_ref[...] + gt_ref[0] * acc
        elif epilogue == "relu2":
            o_ref[...] = jnp.square(jnp.maximum(acc, 0.0)).astype(o_ref.dtype)
        else:
            o_ref[...] = acc.astype(o_ref.dtype)

    if nk == 1:
        finish(part)
    else:
        @pl.when(k == 0)
        def _():
            acc_ref[...] = part

        @pl.when(jnp.logical_and(k > 0, k < nk - 1))
        def _():
            acc_ref[...] += part

        @pl.when(k == nk - 1)
        def _():
            finish(acc_ref[...] + part)


def matmul(a, w, *, layer=None, epilogue="none", out_dtype=F32, resid=None, gate=None,
           rows_per_batch=None, tm=1024, tn=1024, tk=2048):
    M, K = a.shape
    N = w.shape[-1]
    tk = min(tk, K)
    tn = min(tn, N)
    nk = K // tk
    if layer is None:
        w_spec = pl.BlockSpec((tk, tn), lambda i, j, k: (k, j))
    else:
        w_spec = pl.BlockSpec((1, tk, tn), lambda i, j, k: (layer, k, j))
    in_specs = [pl.BlockSpec((tm, tk), lambda i, j, k: (i, k)), w_spec]
    args = [a, w]
    if epilogue == "resid":
        per = rows_per_batch // tm
        in_specs += [
            pl.BlockSpec((tm, tn), lambda i, j, k: (i, j)),
            pl.BlockSpec((1, 1, tn), lambda i, j, k: (i // per, 0, j)),
        ]
        args += [resid, gate]
    acc_shape = (tm, tn) if nk > 1 else (8, LANES)
    return pl.pallas_call(
        functools.partial(_mm_kernel, epilogue, nk),
        out_shape=jax.ShapeDtypeStruct((M, N), out_dtype),
        grid=(M // tm, N // tn, nk),
        in_specs=in_specs,
        out_specs=pl.BlockSpec((tm, tn), lambda i, j, k: (i, j)),
        scratch_shapes=[pltpu.VMEM(acc_shape, F32)],
        compiler_params=_cparams(("arbitrary", "arbitrary", "arbitrary")),
        name="mm_" + epilogue,
    )(*args)


def _pad_rank(w1, w2):
    r = w1.shape[1]
    rp = -(-r // LANES) * LANES
    if rp != r:
        w1 = jnp.pad(w1, ((0, 0), (0, rp - r)))
        w2 = jnp.pad(w2, ((0, rp - r), (0, 0)))
    return w1.astype(BF16), w2.astype(BF16)


def _transpose_lane_groups(xs):
    nl = xs[0].shape[-1]
    n = len(xs)
    width = nl // n
    grp = lax.broadcasted_iota(jnp.int32, xs[0].shape, 1) // width
    ys = []
    for c in range(n):
        acc = None
        for b in range(n):
            shift = ((b - c) * width) % nl
            piece = pltpu.roll(xs[b], shift, axis=1) if shift else xs[b]
            acc = piece if acc is None else jnp.where(grp == b, piece, acc)
        ys.append(acc)
    return ys


def _rows_to_head_lanes(x, nb, o_ref):
    tq = x.shape[0] // nb
    ng = x.shape[1] // LANES
    for g in range(ng):
        xs = [x[b * tq:(b + 1) * tq, g * LANES:(g + 1) * LANES] for b in range(nb)]
        for c, y in enumerate(_transpose_lane_groups(xs)):
            o_ref[c * ng + g] = y


def _head_lanes_to_rows(y_ref, nb):
    ng = y_ref.shape[0] // nb
    cols = [_transpose_lane_groups([y_ref[c * ng + g] for c in range(nb)]) for g in range(ng)]
    return [jnp.concatenate([cols[g][b] for g in range(ng)], axis=1) for b in range(nb)]


def _proj_lora_kernel(act, lora_hl, a_ref, w_ref, al_ref, w1_ref, w2_ref, b_ref, o_ref, ol_ref):
    nb, tq, kdim = a_ref.shape
    acc = jnp.dot(a_ref[...].reshape(nb * tq, kdim), w_ref[...], preferred_element_type=F32)
    _rows_to_head_lanes(acc, nb, o_ref)
    t = jnp.dot(al_ref[...].reshape(nb * tq, kdim), w1_ref[...], preferred_element_type=F32)
    if act == "tanh":
        t = jnp.tanh(t)
    elif act == "sigmoid":
        t = jax.nn.sigmoid(t)
    o = jnp.dot(t.astype(BF16), w2_ref[...], preferred_element_type=F32) + b_ref[...]
    if lora_hl:
        _rows_to_head_lanes(o, nb, ol_ref)
    else:
        ol_ref[...] = o.reshape(nb, tq, o.shape[1])


def proj_lora_head_lanes(a, w, a_lora, w1, w2, bias, act, lora_hl, tq=128):
    B, T, K = a.shape
    D = w.shape[1]
    R = w1.shape[1]
    hl = jax.ShapeDtypeStruct((HEAD_DIM, T, LANES), F32)
    hl_spec = pl.BlockSpec((HEAD_DIM, tq, LANES), lambda i: (0, i, 0))
    row_spec = pl.BlockSpec((B, tq, D), lambda i: (0, i, 0))
    return pl.pallas_call(
        functools.partial(_proj_lora_kernel, act, lora_hl),
        out_shape=[hl, hl if lora_hl else jax.ShapeDtypeStruct((B, T, D), F32)],
        grid=(T // tq,),
        in_specs=[
            pl.BlockSpec((B, tq, K), lambda i: (0, i, 0)),
            pl.BlockSpec((K, D), lambda i: (0, 0)),
            pl.BlockSpec((B, tq, K), lambda i: (0, i, 0)),
            pl.BlockSpec((K, R), lambda i: (0, 0)),
            pl.BlockSpec((R, D), lambda i: (0, 0)),
            pl.BlockSpec((1, D), lambda i: (0, 0)),
        ],
        out_specs=[hl_spec, hl_spec if lora_hl else row_spec],
        compiler_params=_cparams(("arbitrary",)),
        name="proj_lora_" + act,
    )(a, w, a_lora, w1, w2, bias.reshape(1, D))


def _out_proj_hl_kernel(y_ref, g_ref, w_ref, x_ref, gt_ref, o_ref):
    nb, tq, d = g_ref.shape
    ys = _head_lanes_to_rows(y_ref, nb)
    a = jnp.concatenate([(ys[b] * g_ref[b]).astype(BF16) for b in range(nb)], axis=0)
    acc = jnp.dot(a, w_ref[...], preferred_element_type=F32)
    o_ref[...] = x_ref[...] + gt_ref[...] * acc.reshape(nb, tq, acc.shape[1])


def out_proj_head_lanes(y, g, w, x, gate, tq=64):
    B, T, D = g.shape
    N = w.shape[1]
    return pl.pallas_call(
        _out_proj_hl_kernel,
        out_shape=jax.ShapeDtypeStruct((B, T, N), F32),
        grid=(T // tq,),
        in_specs=[
            pl.BlockSpec((HEAD_DIM, tq, LANES), lambda i: (0, i, 0)),
            pl.BlockSpec((B, tq, D), lambda i: (0, i, 0)),
            pl.BlockSpec((D, N), lambda i: (0, 0)),
            pl.BlockSpec((B, tq, N), lambda i: (0, i, 0)),
            pl.BlockSpec((B, 1, N), lambda i: (0, 0, 0)),
        ],
        out_specs=pl.BlockSpec((B, tq, N), lambda i: (0, i, 0)),
        compiler_params=_cparams(("arbitrary",)),
        name="out_proj_head_lanes",
    )(y, g, w, x, gate)


def _scan_kernel(r_ref, wl_ref, k_ref, v_ref, al_ref, kk_ref, ka_ref, rk_ref,
                 lnw_ref, lnb_ref, o_ref,
                 s_ref, v_s, dec_s, kap_s, nka_s, km_s, yt_s):
    dh, tb, nl = r_ref.shape
    half = dh // 2

    @pl.when(pl.program_id(0) == 0)
    def _():
        s_ref[...] = jnp.zeros_like(s_ref)

    def seq(ref):
        return ref[...].reshape(dh, tb, nl)

    def put(ref, val):
        ref[...] = val.reshape(dh * tb, nl)

    put(v_s, v_ref[...])
    z = -wl_ref[...]
    softplus = jnp.maximum(z, 0.0) + jnp.log1p(jnp.exp(-jnp.abs(z)))
    put(dec_s, jnp.exp(-jnp.exp(-softplus - 0.5)))
    a = jax.nn.sigmoid(al_ref[...])
    k = k_ref[...]
    kk = k * kk_ref[...]
    nrm = jnp.sqrt(jnp.sum(kk * kk, axis=0, keepdims=True))
    kap = kk / jnp.maximum(nrm, 1e-12)
    put(kap_s, kap)
    put(nka_s, -(kap * a))
    put(km_s, k * (1 + (a - 1) * ka_ref[...]))

    def row(ref, t, c):
        return jnp.broadcast_to(ref[pl.ds(c * tb + t, 1), :], (half, nl))

    def r_row(t, c):
        return jnp.broadcast_to(r_ref[c, pl.ds(t, 1), :], (half, nl))

    def tile(hf, t):
        return pl.ds(hf * half * tb + t, half, stride=tb)

    zeros = jnp.zeros((half, nl), F32)

    def first_sa(hf):
        def body(g, acc):
            for j in range(SCAN_KU):
                c = g * SCAN_KU + j
                acc = acc + s_ref[hf, c] * row(kap_s, 0, c)
            return acc
        return lax.fori_loop(0, dh // SCAN_KU, body, zeros)

    def run_half(hf, t, t_next, sa):
        v_t = v_s[tile(hf, t), :]

        def body(g, carry):
            y, sa_next = carry
            for j in range(SCAN_KU):
                c = g * SCAN_KU + j
                s_new = (s_ref[hf, c] * row(dec_s, t, c) + sa * row(nka_s, t, c)
                         + v_t * row(km_s, t, c))
                s_ref[hf, c] = s_new
                y = y + s_new * r_row(t, c)
                sa_next = sa_next + s_new * row(kap_s, t_next, c)
            return y, sa_next

        y, sa_next = lax.fori_loop(0, dh // SCAN_KU, body, (zeros, zeros))
        yt_s[pl.ds(pl.multiple_of(t * dh + hf * half, half), half), :] = y
        return sa_next

    def step(t, sa):
        t_next = jnp.minimum(t + 1, tb - 1)
        return run_half(0, t, t_next, sa[0]), run_half(1, t, t_next, sa[1])

    lax.fori_loop(0, tb, step, (first_sa(0), first_sa(1)))

    for ch in range(dh):
        o_ref[ch] = yt_s[pl.ds(ch, tb, stride=dh), :]
    y = o_ref[...]
    mean = jnp.mean(y, axis=0, keepdims=True)
    d = y - mean
    var = jnp.mean(d * d, axis=0, keepdims=True)
    yn = d * lax.rsqrt(var + GN_EPS) * lnw_ref[...] + lnb_ref[...]
    bonus = jnp.sum(r_ref[...] * seq(km_s) * rk_ref[...], axis=0, keepdims=True)
    o_ref[...] = yn + bonus * v_ref[...]


def rwkv_scan(r, wl, k, v, al, k_k, k_a, r_k, ln_w, ln_b):
    dh, T, nl = r.shape
    tb = SCAN_TB
    seq = pl.BlockSpec((dh, tb, nl), lambda i: (0, i, 0))
    par = pl.BlockSpec((dh, 1, nl), lambda i: (0, 0, 0))
    return pl.pallas_call(
        _scan_kernel,
        out_shape=jax.ShapeDtypeStruct((dh, T, nl), F32),
        grid=(T // tb,),
        in_specs=[seq] * 5 + [par] * 5,
        out_specs=seq,
        scratch_shapes=([pltpu.VMEM((2, dh, dh // 2, nl), F32)]
                        + [pltpu.VMEM((dh * tb, nl), F32)] * 6),
        compiler_params=_cparams(("arbitrary",)),
        name="rwkv_scan",
    )(r, wl, k, v, al, k_k, k_a, r_k, ln_w, ln_b)


def _bias_kernel(g_ref, o_ref):
    nq, nw = o_ref.shape[1], o_ref.shape[2]
    ng = g_ref.shape[2]
    row = lax.broadcasted_iota(jnp.int32, (nq, ng), 0)
    x = jnp.broadcast_to(g_ref[0], (nq, ng))
    shift = 1
    while shift < nq:
        x = jnp.where((row & shift) != 0, pltpu.roll(x, shift, axis=1), x)
        shift *= 2
    qi = lax.broadcasted_iota(jnp.int32, (nq, nw), 0) // CHUNK
    kj = lax.broadcasted_iota(jnp.int32, (nq, nw), 1) // CHUNK
    band = jnp.logical_and(kj >= qi, kj <= qi + N_LEFT_CHUNKS)
    o_ref[0] = jnp.where(band, x[:, :nw], MASK_VALUE)


def band_bias(rel_bias):
    H = rel_bias.shape[0]
    ng = -(-(ATTN_W + ATTN_Q) // LANES) * LANES
    far = rel_bias[:, 2 * REL_CLIP:]
    near = rel_bias[:, 2 * REL_CLIP - 1:REL_CLIP + ATTN_PAD - ATTN_W:-1]
    n_far = ATTN_W - near.shape[1]
    g = jnp.concatenate(
        [jnp.broadcast_to(far, (H, n_far)), near, jnp.broadcast_to(far, (H, ng - ATTN_W))], axis=1)
    return pl.pallas_call(
        _bias_kernel,
        out_shape=jax.ShapeDtypeStruct((H, ATTN_Q, ATTN_W), F32),
        grid=(H,),
        in_specs=[pl.BlockSpec((1, 1, ng), lambda h: (h, 0, 0))],
        out_specs=pl.BlockSpec((1, ATTN_Q, ATTN_W), lambda h: (h, 0, 0)),
        compiler_params=_cparams(("arbitrary",)),
        name="band_bias",
    )(g.reshape(H, 1, ng))


def _attn_kernel(q_ref, k_ref, v_ref, b_ref, o_ref):
    T = q_ref.shape[1]
    n_steps = T // ATTN_Q
    n_masked = -(-ATTN_PAD // ATTN_Q)
    lane = lax.broadcasted_iota(jnp.int32, (ATTN_Q, LANES), 1)
    scale = HEAD_DIM ** -0.5
    head_scale = [jnp.where(lane < HEAD_DIM, scale, 0.0).astype(BF16),
                  jnp.where(lane >= HEAD_DIM, scale, 0.0).astype(BF16)]
    kpos = lax.broadcasted_iota(jnp.int32, (2 * ATTN_Q, ATTN_W), 1)

    def step(n, masked):
        start = pl.multiple_of(n * ATTN_Q, ATTN_Q)
        q2 = q_ref[0, pl.ds(start, ATTN_Q), :]
        kw = k_ref[0, pl.ds(start, ATTN_W), :]
        vw = v_ref[0, pl.ds(start, ATTN_W), :]
        qs = jnp.concatenate([q2 * head_scale[0], q2 * head_scale[1]], axis=0)
        s = lax.dot_general(qs, kw, (((1,), (1,)), ((), ())), preferred_element_type=F32)
        s = s + b_ref[...].reshape(2 * ATTN_Q, ATTN_W)
        if masked:
            s = jnp.where(kpos + start >= ATTN_PAD, s, MASK_VALUE)
        m = jnp.max(s, axis=-1, keepdims=True)
        p = jnp.exp(s - m)
        l = jnp.sum(p, axis=-1, keepdims=True)
        o = jnp.dot(p.astype(BF16), vw, preferred_element_type=F32) / l
        o_ref[0, pl.ds(start, ATTN_Q), :] = jnp.where(
            lane < HEAD_DIM, o[:ATTN_Q], o[ATTN_Q:]).astype(o_ref.dtype)

    def masked_body(n, carry):
        step(n, True)
        return carry

    def body(n, carry):
        step(n, False)
        return carry

    lax.fori_loop(0, n_masked, masked_body, 0, unroll=4)
    lax.fori_loop(n_masked, n_steps, body, 0, unroll=6)


def band_attention(q, kvp, bias):
    B, T, D = q.shape
    Tp = kvp.shape[1]
    n_pairs = D // LANES
    return pl.pallas_call(
        _attn_kernel,
        out_shape=jax.ShapeDtypeStruct((B, T, D), BF16),
        grid=(B, n_pairs),
        in_specs=[
            pl.BlockSpec((1, T, LANES), lambda b, p: (b, 0, p)),
            pl.BlockSpec((1, Tp, LANES), lambda b, p: (b, 0, p)),
            pl.BlockSpec((1, Tp, LANES), lambda b, p: (b, 0, p + n_pairs)),
            pl.BlockSpec((2, ATTN_Q, ATTN_W), lambda b, p: (p, 0, 0)),
        ],
        out_specs=pl.BlockSpec((1, T, LANES), lambda b, p: (b, 0, p)),
        compiler_params=_cparams(("arbitrary", "arbitrary")),
        name="band_attention",
    )(q, kvp, kvp, bias)


def _group_cols(w, H, group):
    lead = w.shape[:-1]
    w = w.reshape(lead + (H, HEAD_DIM // group, group))
    w = jnp.moveaxis(w, -3, -1)
    return w.reshape(lead + (H * HEAD_DIM,))


def _param_head_lanes(p, B, H, group):
    t = jnp.transpose(p.astype(F32).reshape(H, HEAD_DIM // group, group), (2, 1, 0))
    return jnp.tile(t.reshape(HEAD_DIM, H), (1, B)).reshape(HEAD_DIM, 1, B * H)


def kernel(x, c, w_ada, b_ada, g_mix, g_mlp, w_up, w_down, rwkv_mu, rwkv_w_r, rwkv_w_k, rwkv_w_v, rwkv_w_o, rwkv_w0, rwkv_w1, rwkv_w2, rwkv_a0, rwkv_a1, rwkv_a2, rwkv_g1, rwkv_g2, rwkv_k_k, rwkv_k_a, rwkv_r_k, rwkv_ln_w, rwkv_ln_b, attn_w_q, attn_w_o, attn_rel_bias, w_ada_kv, b_ada_kv, g_kv, w_k_shared, w_v_shared, g_final):
    B, T, D = x.shape
    H = D // HEAD_DIM
    M = B * T
    depth = w_ada.shape[0]
    n_a = rwkv_mu.shape[0]
    group = LANES // H
    assert T % ATTN_Q == 0 and T % SCAN_TB == 0 and B * H == LANES and group == B

    xf = x.reshape(M, D)
    c_pad = jnp.pad(c, ((0, 16 - B), (0, 0))).astype(BF16)
    mods = ada_mod(c_pad, w_ada, b_ada)[:, :B]
    mod_kv = ada_mod(c_pad, w_ada_kv[None], b_ada_kv[None])[0, :B]

    def split(m, n):
        return [p.reshape(B, 1, D) for p in jnp.split(m, n, axis=-1)]

    w_up_bf = w_up.astype(BF16)
    w_down_bf = w_down.astype(BF16)
    kvp = None
    for layer in range(depth):
        sh1, sc1, gt1, sh2, sc2, gt2 = split(mods[layer], 6)
        if layer == n_a:
            sh_kv, sc_kv = split(mod_kv, 2)
            h_kv = norm_mod(xf, g_kv, sh_kv, sc_kv, T)
            w_kv = jnp.concatenate([w_k_shared, w_v_shared], axis=1).astype(BF16)
            kv = matmul(h_kv, w_kv, out_dtype=BF16, tn=2048).reshape(B, T, 2 * D)
            kvp = jnp.pad(kv, ((0, 0), (ATTN_PAD, 0), (0, 0)))
        if layer < n_a:
            i = layer
            gc = functools.partial(_group_cols, H=H, group=group)
            hp = functools.partial(_param_head_lanes, B=B, H=H, group=group)
            xr, xw, xk, xv, xa, xg = premix(xf, g_mix[layer], sh1, sc1, rwkv_mu[i], T)
            rows = lambda a: a.reshape(B, T, D)
            r, g = proj_lora_head_lanes(
                rows(xr), gc(rwkv_w_r[i]).astype(BF16), rows(xg),
                *_pad_rank(rwkv_g1[i], gc(rwkv_g2[i])), jnp.zeros((D,), F32), "sigmoid", False)
            k, wl = proj_lora_head_lanes(
                rows(xk), gc(rwkv_w_k[i]).astype(BF16), rows(xw),
                *_pad_rank(rwkv_w1[i], gc(rwkv_w2[i])), gc(rwkv_w0[i]), "tanh", True)
            v, al = proj_lora_head_lanes(
                rows(xv), gc(rwkv_w_v[i]).astype(BF16), rows(xa),
                *_pad_rank(rwkv_a1[i], gc(rwkv_a2[i])), gc(rwkv_a0[i]), "none", True)
            y = rwkv_scan(r, wl, k, v, al,
                          hp(rwkv_k_k[i]), hp(rwkv_k_a[i]), hp(rwkv_r_k[i].reshape(D)),
                          hp(rwkv_ln_w[i]), hp(rwkv_ln_b[i]))
            w_o = gc(rwkv_w_o[i].T).T
            xf = out_proj_head_lanes(y, g, w_o.astype(BF16), rows(xf), gt1).reshape(M, D)
        else:
            i = layer - n_a
            h = norm_mod(xf, g_mix[layer], sh1, sc1, T)
            q = matmul(h, attn_w_q[i].astype(BF16), out_dtype=BF16, **SQUARE_TILES).reshape(B, T, D)
            bias = band_bias(attn_rel_bias[i].astype(F32))
            yg = band_attention(q, kvp, bias).reshape(M, D)
            xf = matmul(yg, attn_w_o[i].astype(BF16), epilogue="resid", resid=xf, gate=gt1,
                        rows_per_batch=T, **SQUARE_TILES)
        h = norm_mod(xf, g_mlp[layer], sh2, sc2, T)
        hid = matmul(h, w_up_bf, layer=layer, epilogue="relu2", out_dtype=BF16, tn=2048)
        xf = matmul(hid, w_down_bf, layer=layer, epilogue="resid", resid=xf, gate=gt2,
                    rows_per_batch=T)
    return final_norm(xf, g_final).reshape(B, T, D)
```
